```python
import math
import jax
import jax.numpy as jnp
from jax import lax
import numpy as np

D_MODEL = 1024
BATCH = 8
SEQ = 2048
DEPTH = 4
DEC_BATCH = 128
DEC_SEQ = 1
PAST_LEN = 8192
PAGE_SIZE = 128

N_A = DEPTH // 2
N_B = DEPTH - N_A
D_A = D_MODEL
DK_A = 128
H_A = D_A // DK_A
DV_A = D_A // H_A
CHUNK_A = 32
LB_FLOOR = 1e-30
H_B = D_MODEL // 128
NOPE = 128
ROPE_DIM = 64
V_DIM = 128
Q_LORA = 384
KV_LORA = 256
ROPE_THETA = 10000.0
Q_BLOCK = 128
SCALE = (NOPE + ROPE_DIM) ** -0.5
MASK_VALUE = -1e30
D_FF = ((8 * D_MODEL // 3 + 255) // 256) * 256
EPS = 1e-6

kernel_name = 'yoco_hgrn2_mla_decoder_step'


def rmsnorm(x, gain):
    xf = x.astype(jnp.float32)
    y = xf * lax.rsqrt(jnp.mean(xf * xf, axis=-1, keepdims=True) + EPS)
    return (y * gain.astype(jnp.float32)).astype(x.dtype)


def rope(x, pos):
    half = x.shape[-1] // 2
    inv = ROPE_THETA ** (-jnp.arange(half, dtype=jnp.float32) / half)
    ang = pos.astype(jnp.float32)[:, None] * inv[None, :]
    shape = (1, x.shape[1]) + (1,) * (x.ndim - 3) + (half,)
    cos = jnp.cos(ang).reshape(shape)
    sin = jnp.sin(ang).reshape(shape)
    xf = x.astype(jnp.float32)
    x1, x2 = xf[..., :half], xf[..., half:]
    return jnp.concatenate([x1 * cos - x2 * sin, x2 * cos + x1 * sin], axis=-1).astype(x.dtype)


def swiglu(h, w_in, w_out):
    g, u = jnp.split(h @ w_in, 2, axis=-1)
    return (jax.nn.silu(g) * u) @ w_out


def hgrn_lower_bounds(lb_logits):
    p = jax.nn.softmax(lb_logits.astype(jnp.float32), axis=0)
    return jnp.cumsum(p, axis=0) - p[0:1]


def hgrn2_chunked(q, k, v, log_f, s0, chunk):
    B, T, H, _ = q.shape
    DV = v.shape[-1]
    n = T // chunk

    def to_chunks(a):
        return a.reshape(B, n, chunk, H, a.shape[-1]).transpose(1, 0, 3, 2, 4)

    tril = jnp.tril(jnp.ones((chunk, chunk), dtype=bool))

    def step(s, inp):
        qc, kc, vc, gc = inp
        b = jnp.cumsum(gc, axis=2)
        diff = b[:, :, :, None, :] - b[:, :, None, :, :]
        decay = jnp.where(tril[:, :, None], jnp.exp(jnp.minimum(diff, 0.0)), 0.0)
        scores = jnp.einsum('bhtk,bhsk,bhtsk->bhts', qc, kc, decay)
        o = (jnp.einsum('bhts,bhsv->bhtv', scores, vc)
             + jnp.einsum('bhtk,bhkv->bhtv', qc * jnp.exp(b), s))
        b_last = b[:, :, -1:, :]
        s = (jnp.exp(b_last[:, :, 0, :, None]) * s
             + jnp.einsum('bhsk,bhsv->bhkv', kc * jnp.exp(b_last - b), vc))
        return s, o

    s, o = lax.scan(step, s0, (to_chunks(q), to_chunks(k), to_chunks(v), to_chunks(log_f)))
    o = o.transpose(1, 0, 3, 2, 4).reshape(B, T, H, DV)
    return o, s


def hgrn2_mixer(h, s0, lb, w_in, g_norm, w_out):
    B, T, _ = h.shape
    q, fz, i, gz = jnp.split(h @ w_in, 4, axis=-1)

    def heads(a):
        return a.astype(jnp.float32).reshape(B, T, H_A, -1)

    q = jax.nn.silu(heads(q)) * (DK_A ** -0.5)
    lbh = lb.reshape(H_A, DK_A)
    log_lb = jnp.log(jnp.maximum(lbh, LB_FLOOR))
    log_f = jnp.logaddexp(log_lb, jnp.log1p(-lbh) + jax.nn.log_sigmoid(heads(fz)))
    log_f = jnp.minimum(log_f, 0.0)
    k = -jnp.expm1(log_f)
    v = heads(i)
    o, s = hgrn2_chunked(q, k, v, log_f, s0.astype(jnp.float32), math.gcd(T, CHUNK_A))
    o = o * lax.rsqrt(jnp.mean(o * o, axis=-1, keepdims=True) + EPS)
    o = o * g_norm.astype(jnp.float32).reshape(H_A, DV_A) * jax.nn.silu(heads(gz))
    return o.reshape(B, T, D_A).astype(h.dtype) @ w_out, s


def mla_shared_kv(x, pos, kv_norm, w_dkv, kv_a_norm):
    ckr = rmsnorm(x, kv_norm) @ w_dkv
    c = rmsnorm(ckr[..., :KV_LORA], kv_a_norm)
    kr = rope(ckr[..., KV_LORA:], pos)
    return c, kr


def mla_attend(q_lat, q_rope, c_all, kr_all, q_pos, k_pos):
    B, T = q_lat.shape[0], q_lat.shape[1]

    def block(qb):
        ql, qr, qp = qb
        s = (jnp.einsum('bthc,bsc->bhts', ql, c_all)
             + jnp.einsum('bthr,bsr->bhts', qr, kr_all)).astype(jnp.float32) * SCALE
        s = jnp.where(k_pos[None, None, None, :] <= qp[None, None, :, None], s, MASK_VALUE)
        p = jax.nn.softmax(s, axis=-1).astype(c_all.dtype)
        return jnp.einsum('bhts,bsc->bthc', p, c_all)

    if T > Q_BLOCK and T % Q_BLOCK == 0:
        nb = T // Q_BLOCK

        def split(a):
            return a.reshape((B, nb, Q_BLOCK) + a.shape[2:]).swapaxes(0, 1)

        out = lax.map(block, (split(q_lat), split(q_rope), q_pos.reshape(nb, Q_BLOCK)))
        return out.swapaxes(0, 1).reshape(B, T, H_B, KV_LORA)
    return block((q_lat, q_rope, q_pos))


def mla_mixer(h, pos, c_all, kr_all, k_pos, w_ukv, w_dq, q_a_norm, w_uq, w_out):
    B, T, _ = h.shape
    q = (rmsnorm(h @ w_dq, q_a_norm) @ w_uq).reshape(B, T, H_B, NOPE + ROPE_DIM)
    q_nope, q_rope = q[..., :NOPE], rope(q[..., NOPE:], pos)
    w_ukv_h = w_ukv.reshape(KV_LORA, H_B, NOPE + V_DIM)
    q_lat = jnp.einsum('bthn,chn->bthc', q_nope, w_ukv_h[..., :NOPE])
    o_lat = mla_attend(q_lat, q_rope, c_all, kr_all, pos, k_pos)
    o = jnp.einsum('bthc,chv->bthv', o_lat, w_ukv_h[..., NOPE:]).reshape(B, T, H_B * V_DIM)
    return o @ w_out


def run_trunk(x, hgrn_states, c_past, kr_past, pos,
              norm_gains, w_ffn_in, w_ffn_out, w_in_a, lb_logits, g_norm_a, w_out_a,
              kv_norm, w_dkv, kv_a_norm, w_ukv, w_dq, q_a_norm, w_uq, w_out_b):
    lb = hgrn_lower_bounds(lb_logits)
    new_states = []
    c_new = kr_new = c_all = kr_all = k_pos = None
    for l in range(DEPTH):
        g = norm_gains[l]
        h = rmsnorm(x, g[0])
        if l < N_A:
            mix, s = hgrn2_mixer(h, hgrn_states[l], lb[l], w_in_a[l], g_norm_a[l], w_out_a[l])
            new_states.append(s)
        else:
            if l == N_A:
                c_new, kr_new = mla_shared_kv(x, pos, kv_norm, w_dkv, kv_a_norm)
                if c_past is None:
                    c_all, kr_all, k_pos = c_new, kr_new, pos
                else:
                    past_len = c_past.shape[1]
                    c_all = jnp.concatenate([c_past.astype(c_new.dtype), c_new], axis=1)
                    kr_all = jnp.concatenate([kr_past.astype(kr_new.dtype), kr_new], axis=1)
                    k_pos = jnp.concatenate([jnp.arange(past_len, dtype=jnp.int32), pos])
            j = l - N_A
            mix = mla_mixer(h, pos, c_all, kr_all, k_pos, w_ukv, w_dq[j], q_a_norm[j], w_uq[j], w_out_b[j])
        x = x + rmsnorm(mix, g[1])
        x = x + rmsnorm(swiglu(rmsnorm(x, g[2]), w_ffn_in[l], w_ffn_out[l]), g[3])
    return x, jnp.stack(new_states), c_new, kr_new


def setup_inputs(seed: int = 0) -> dict:
    key = jax.random.key(seed)
    ks = jax.random.split(key, 24)
    f32 = jnp.float32

    def nrm(k, shape, fan_in):
        return jax.random.normal(k, shape, f32) * (fan_in ** -0.5)

    def gain(k, shape):
        return 1.0 + 0.05 * jax.random.normal(k, shape, f32)

    n_pages = PAST_LEN // PAGE_SIZE
    n_used = DEC_BATCH * n_pages
    n_pool = n_used + max(1, n_used // 4)
    page_table = jax.random.permutation(ks[5], n_pool)[:n_used].reshape(DEC_BATCH, n_pages).astype(jnp.int32)

    return {
        'x_prompt': jax.random.normal(ks[0], (BATCH, SEQ, D_MODEL), f32),
        'x_sample': jax.random.normal(ks[1], (DEC_BATCH, DEC_SEQ, D_MODEL), f32),
        'state_hgrn': 0.3 * jax.random.normal(ks[2], (N_A, DEC_BATCH, H_A, DK_A, DV_A), f32),
        'cache_kv_latent': jax.random.normal(ks[3], (n_pool, PAGE_SIZE, KV_LORA), f32),
        'cache_k_rope': jax.random.normal(ks[4], (n_pool, PAGE_SIZE, ROPE_DIM), f32),
        'page_table': page_table,
        'norm_gains': gain(ks[6], (DEPTH, 4, D_MODEL)),
        'w_ffn_in': nrm(ks[7], (DEPTH, D_MODEL, 2 * D_FF), D_MODEL),
        'w_ffn_out': nrm(ks[8], (DEPTH, D_FF, D_MODEL), D_FF),
        'w_in_a': nrm(ks[9], (N_A, D_MODEL, 4 * D_A), D_MODEL),
        'lb_logits': 0.5 * jax.random.normal(ks[10], (N_A, D_A), f32),
        'g_norm_a': gain(ks[11], (N_A, D_A)),
        'w_out_a': nrm(ks[12], (N_A, D_A, D_MODEL), D_A),
        'kv_norm': gain(ks[13], (D_MODEL,)),
        'w_dkv': nrm(ks[14], (D_MODEL, KV_LORA + ROPE_DIM), D_MODEL),
        'kv_a_norm': gain(ks[15], (KV_LORA,)),
        'w_ukv': nrm(ks[16], (KV_LORA, H_B * (NOPE + V_DIM)), KV_LORA),
        'w_dq': nrm(ks[17], (N_B, D_MODEL, Q_LORA), D_MODEL),
        'q_a_norm': gain(ks[18], (N_B, Q_LORA)),
        'w_uq': nrm(ks[19], (N_B, Q_LORA, H_B * (NOPE + ROPE_DIM)), Q_LORA),
        'w_out_b': nrm(ks[20], (N_B, H_B * V_DIM, D_MODEL), H_B * V_DIM),
    }


def reference(x_prompt, x_sample, state_hgrn, cache_kv_latent, cache_k_rope, page_table,
              norm_gains, w_ffn_in, w_ffn_out, w_in_a, lb_logits, g_norm_a, w_out_a,
              kv_norm, w_dkv, kv_a_norm, w_ukv, w_dq, q_a_norm, w_uq, w_out_b):
    weights = (norm_gains, w_ffn_in, w_ffn_out, w_in_a, lb_logits, g_norm_a, w_out_a,
               kv_norm, w_dkv, kv_a_norm, w_ukv, w_dq, q_a_norm, w_uq, w_out_b)
    pos_p = jnp.arange(x_prompt.shape[1], dtype=jnp.int32)
    s0_p = jnp.zeros((N_A, x_prompt.shape[0], H_A, DK_A, DV_A), jnp.float32)
    y_p, st_p, c_p, kr_p = run_trunk(x_prompt, s0_p, None, None, pos_p, *weights)
    db, n_pages = page_table.shape
    past_len = n_pages * cache_kv_latent.shape[1]
    c_past = cache_kv_latent[page_table].reshape(db, past_len, KV_LORA)
    kr_past = cache_k_rope[page_table].reshape(db, past_len, ROPE_DIM)
    pos_s = past_len + jnp.arange(x_sample.shape[1], dtype=jnp.int32)
    y_s, st_s, c_s, kr_s = run_trunk(x_sample, state_hgrn, c_past, kr_past, pos_s, *weights)
    return (y_p, y_s, st_p, c_p, kr_p, st_s, c_s, kr_s)
```

```python
import functools

import jax
import jax.numpy as jnp
from jax import lax
from jax.experimental import pallas as pl
from jax.experimental.pallas import tpu as pltpu

F32 = jnp.float32
BF16 = jnp.bfloat16

D_MODEL = 1024
N_HEADS = 8
HEAD_DIM = 128
ROPE_DIM = 64
Q_LORA = 384
KV_LORA = 256
D_FF = 2816
N_HGRN = 2
EPS = 1e-6
LB_FLOOR = 1e-30
ROPE_THETA = 10000.0
ATTN_SCALE = (HEAD_DIM + ROPE_DIM) ** -0.5
MASK_VALUE = -1e30

HGRN_CHUNK = 128
FFN_COLS = 256
ATTN_TQ = 128
ATTN_TK = 256
PAGES_PER_STEP = 8
VMEM_LIMIT = 56 * 1024 * 1024

_NT = (((1,), (1,)), ((), ()))
_TN = (((0,), (0,)), ((), ()))


def _params(*sem):
    return pltpu.CompilerParams(dimension_semantics=sem, vmem_limit_bytes=VMEM_LIMIT)


def _dot(a, b):
    return jnp.dot(a, b, preferred_element_type=F32)


def _dot_nt(a, b):
    return lax.dot_general(a, b, _NT, preferred_element_type=F32)


def _rms(xf, gain):
    ms = jnp.mean(xf * xf, axis=-1, keepdims=True)
    return xf * lax.rsqrt(ms + EPS) * gain


def _silu(x):
    return x * jax.nn.sigmoid(x)


def _const_spec(shape):
    n = len(shape)
    return pl.BlockSpec(shape, lambda *_: (0,) * n)


def _norm_mm_kernel(x_ref, g_ref, w_ref, o_ref, *, cols):
    h = _rms(x_ref[...], g_ref[...]).astype(BF16)
    for c in range(0, o_ref.shape[1], cols):
        o_ref[:, c:c + cols] = _dot(h, w_ref[:, c:c + cols])


def _norm_mm(x, gain, w, tm):
    m, d = x.shape
    n = w.shape[1]
    return pl.pallas_call(
        functools.partial(_norm_mm_kernel, cols=1024),
        grid=(m // tm,),
        in_specs=[pl.BlockSpec((tm, d), lambda i: (i, 0)),
                  _const_spec((1, d)), _const_spec((d, n))],
        out_specs=pl.BlockSpec((tm, n), lambda i: (i, 0)),
        out_shape=jax.ShapeDtypeStruct((m, n), F32),
        compiler_params=_params("parallel"),
        name="norm_mm",
    )(x, gain.reshape(1, d), w)


def _mm_post_kernel(a_ref, w_ref, g_ref, x_ref, o_ref):
    y = _dot(a_ref[...], w_ref[...])
    o_ref[...] = x_ref[...] + _rms(y, g_ref[...])


def _mm_post(a, w, gain, x, tm):
    m, k = a.shape
    d = w.shape[1]
    return pl.pallas_call(
        _mm_post_kernel,
        grid=(m // tm,),
        in_specs=[pl.BlockSpec((tm, k), lambda i: (i, 0)),
                  _const_spec((k, d)), _const_spec((1, d)),
                  pl.BlockSpec((tm, d), lambda i: (i, 0))],
        out_specs=pl.BlockSpec((tm, d), lambda i: (i, 0)),
        out_shape=jax.ShapeDtypeStruct((m, d), F32),
        compiler_params=_params("parallel"),
        name="mm_post",
    )(a, w, gain.reshape(1, d), x)


def _ffn_kernel(x_ref, g_in_ref, g_out_ref, w_in_ref, w_out_ref, o_ref, a_scr):
    x = x_ref[...]
    h = _rms(x, g_in_ref[...]).astype(BF16)
    for c in range(0, D_FF, FFN_COLS):
        g = _dot(h, w_in_ref[:, c:c + FFN_COLS])
        u = _dot(h, w_in_ref[:, D_FF + c:D_FF + c + FFN_COLS])
        a_scr[:, c:c + FFN_COLS] = (_silu(g) * u).astype(BF16)
    y = _dot(a_scr[...], w_out_ref[...])
    o_ref[...] = x + _rms(y, g_out_ref[...])


def _ffn(x, g_in, g_out, w_in, w_out, tm):
    m, d = x.shape
    return pl.pallas_call(
        _ffn_kernel,
        grid=(m // tm,),
        in_specs=[pl.BlockSpec((tm, d), lambda i: (i, 0)),
                  _const_spec((1, d)), _const_spec((1, d)),
                  _const_spec((d, 2 * D_FF)), _const_spec((D_FF, d))],
        out_specs=pl.BlockSpec((tm, d), lambda i: (i, 0)),
        out_shape=jax.ShapeDtypeStruct((m, d), F32),
        scratch_shapes=[pltpu.VMEM((tm, D_FF), BF16)],
        compiler_params=_params("parallel"),
        name="ffn",
    )(x, g_in.reshape(1, d), g_out.reshape(1, d), w_in, w_out)


def _lower_bound(logits, layer):
    m = jnp.max(logits, axis=0, keepdims=True)
    e = jnp.exp(logits - m)
    p = e / jnp.sum(e, axis=0, keepdims=True)
    acc = p[0:1]
    for r in range(1, layer + 1):
        acc = acc + p[r:r + 1]
    return acc - p[0:1]


def _forget_gate(fz, lb):
    f = jnp.maximum(lb, LB_FLOOR) + (1.0 - lb) * jax.nn.sigmoid(fz)
    return jnp.minimum(f, 1.0)


def _mid_rows(b, m):
    c = b.shape[0]
    if m >= 4:
        g3 = b.reshape(c // (2 * m), 2 * m, b.shape[1])
        return jnp.broadcast_to(g3[:, m - 1:m, :], g3.shape).reshape(b.shape)
    g3 = b.reshape(c // 8, 8, b.shape[1])
    sub = lax.broadcasted_iota(jnp.int32, g3.shape, 1)

    def row(r):
        return jnp.broadcast_to(g3[:, r:r + 1, :], g3.shape)

    if m == 2:
        out = jnp.where(sub < 4, row(1), row(5))
    else:
        out = jnp.where(sub < 2, row(0), jnp.where(sub < 4, row(2), jnp.where(sub < 6, row(4), row(6))))
    return out.reshape(b.shape)


def _hgrn_prompt_kernel(z_ref, lbl_ref, gn_ref, o_ref, st_ref, state_scr, *, layer):
    c = pl.program_id(1)
    chunk = z_ref.shape[0]

    @pl.when(c == 0)
    def _():
        state_scr[...] = jnp.zeros_like(state_scr)

    lb_all = _lower_bound(lbl_ref[...], layer)
    ti = lax.broadcasted_iota(jnp.int32, (chunk, chunk), 0)
    si = lax.broadcasted_iota(jnp.int32, (chunk, chunk), 1)
    level = jnp.where(ti > si, ti ^ si, 0)
    eye = ti == si
    tril = (ti >= si).astype(F32)

    for h in range(N_HEADS):
        lo = h * HEAD_DIM
        hi = lo + HEAD_DIM
        qz = z_ref[:, lo:hi]
        fz = z_ref[:, D_MODEL + lo:D_MODEL + hi]
        v = z_ref[:, 2 * D_MODEL + lo:2 * D_MODEL + hi]
        gz = z_ref[:, 3 * D_MODEL + lo:3 * D_MODEL + hi]

        f = _forget_gate(fz, lb_all[:, lo:hi])
        g = jnp.log(f)
        k = 1.0 - f
        q = _silu(qz) * (HEAD_DIM ** -0.5)
        vb = v.astype(BF16)
        b = jnp.dot(tril, g, precision=lax.Precision.HIGHEST, preferred_element_type=F32)

        a = jnp.where(eye, _dot_nt(q.astype(BF16), k.astype(BF16)), 0.0)
        m = 1
        while m < chunk:
            e = jnp.exp(-jnp.abs(b - _mid_rows(b, m)))
            a = jnp.where(level >= m, _dot_nt((q * e).astype(BF16), (k * e).astype(BF16)), a)
            m *= 2

        st = state_scr[h]
        o = _dot(a.astype(BF16), vb) + _dot_nt((q * jnp.exp(b)).astype(BF16), st.astype(BF16))
        b_last = b[chunk - 1:chunk, :]
        kd = (k * jnp.exp(b_last - b)).astype(BF16)
        state_scr[h] = jnp.exp(b_last) * st + lax.dot_general(vb, kd, _TN, preferred_element_type=F32)

        o = o * lax.rsqrt(jnp.mean(o * o, axis=-1, keepdims=True) + EPS)
        o_ref[:, lo:hi] = (o * gn_ref[:, lo:hi] * _silu(gz)).astype(BF16)

    @pl.when(c == pl.num_programs(1) - 1)
    def _():
        for h in range(N_HEADS):
            st_ref[0, h] = state_scr[h].T


def _hgrn_prompt(z, lb_logits, g_norm, layer, batch, seq):
    n_chunks = seq // HGRN_CHUNK
    return pl.pallas_call(
        functools.partial(_hgrn_prompt_kernel, layer=layer),
        grid=(batch, n_chunks),
        in_specs=[pl.BlockSpec((HGRN_CHUNK, 4 * D_MODEL), lambda b, c: (b * n_chunks + c, 0)),
                  _const_spec(lb_logits.shape), _const_spec((1, D_MODEL))],
        out_specs=[pl.BlockSpec((HGRN_CHUNK, D_MODEL), lambda b, c: (b * n_chunks + c, 0)),
                   pl.BlockSpec((1, N_HEADS, HEAD_DIM, HEAD_DIM), lambda b, c: (b, 0, 0, 0))],
        out_shape=[jax.ShapeDtypeStruct((batch * seq, D_MODEL), BF16),
                   jax.ShapeDtypeStruct((batch, N_HEADS, HEAD_DIM, HEAD_DIM), F32)],
        scratch_shapes=[pltpu.VMEM((N_HEADS, HEAD_DIM, HEAD_DIM), F32)],
        compiler_params=_params("parallel", "arbitrary"),
        name="hgrn_prompt",
    )(z, lb_logits, g_norm.reshape(1, D_MODEL))


SAMPLE_GROUP = 4


def _hgrn_sample_kernel(z_ref, s_ref, lbl_ref, gn_ref, o_ref, so_ref, *, layer):
    lb = _lower_bound(lbl_ref[...], layer).reshape(N_HEADS, HEAD_DIM)
    z3 = z_ref[...].reshape(SAMPLE_GROUP * 4, N_HEADS, HEAD_DIM)
    part = lax.broadcasted_iota(jnp.int32, z3.shape, 0) % 4
    cols3 = jnp.where(part == 0, _silu(z3) * (HEAD_DIM ** -0.5),
                      jnp.where(part == 1, _forget_gate(z3, lb[None]), 0.0))
    cols = cols3.reshape(SAMPLE_GROUP * 4 * N_HEADS, HEAD_DIM).T
    gn = gn_ref[...]
    for i in range(SAMPLE_GROUP):
        for h in range(N_HEADS):
            base = i * 4 * N_HEADS + h
            q_col = cols[:, base:base + 1]
            f_col = cols[:, base + N_HEADS:base + N_HEADS + 1]
            v_row = z_ref[i, 2 * N_HEADS + h:2 * N_HEADS + h + 1, :]
            gz_row = z_ref[i, 3 * N_HEADS + h:3 * N_HEADS + h + 1, :]
            s_new = f_col * s_ref[0, i, h] + (1.0 - f_col) * v_row
            so_ref[0, i, h] = s_new
            o = jnp.sum(q_col * s_new, axis=0, keepdims=True)
            o = o * lax.rsqrt(jnp.mean(o * o, axis=-1, keepdims=True) + EPS)
            o_ref[i, h:h + 1, :] = o * gn[h:h + 1, :] * _silu(gz_row)


def _hgrn_sample(z, state, lb_logits, g_norm, layer):
    n = z.shape[0]
    o, s_new = pl.pallas_call(
        functools.partial(_hgrn_sample_kernel, layer=layer),
        grid=(n // SAMPLE_GROUP,),
        in_specs=[pl.BlockSpec((SAMPLE_GROUP, 4 * N_HEADS, HEAD_DIM), lambda i: (i, 0, 0)),
                  pl.BlockSpec((1, SAMPLE_GROUP, N_HEADS, HEAD_DIM, HEAD_DIM), lambda i: (layer, i, 0, 0, 0)),
                  _const_spec(lb_logits.shape), _const_spec((N_HEADS, HEAD_DIM))],
        out_specs=[pl.BlockSpec((SAMPLE_GROUP, N_HEADS, HEAD_DIM), lambda i: (i, 0, 0)),
                   pl.BlockSpec((1, SAMPLE_GROUP, N_HEADS, HEAD_DIM, HEAD_DIM), lambda i: (0, i, 0, 0, 0))],
        out_shape=[jax.ShapeDtypeStruct((n, N_HEADS, HEAD_DIM), F32),
                   jax.ShapeDtypeStruct((1, n, N_HEADS, HEAD_DIM, HEAD_DIM), F32)],
        compiler_params=_params("parallel"),
        name="hgrn_sample",
    )(z.reshape(n, 4 * N_HEADS, HEAD_DIM), state, lb_logits, g_norm.reshape(N_HEADS, HEAD_DIM))
    return o.reshape(n, D_MODEL), s_new[0]


def _kv_kernel(x_ref, g_ref, wc_ref, wkr_ref, wkrs_ref, ga_ref, cs_ref, sn_ref,
               c_ref, kr_ref, cb_ref, krb_ref):
    h = _rms(x_ref[...], g_ref[...]).astype(BF16)
    c = _rms(_dot(h, wc_ref[...]), ga_ref[...])
    kr = _dot(h, wkr_ref[...]) * cs_ref[...] + _dot(h, wkrs_ref[...]) * sn_ref[...]
    c_ref[...] = c
    kr_ref[...] = kr
    cb_ref[...] = c.astype(BF16)
    krb_ref[...] = kr.astype(BF16)


def _kv_shared(x, gain, w_c, w_kr, w_kr_swapped, gain_a, cs, sn, tm):
    m, d = x.shape
    n_pos = cs.shape[0] // tm
    return pl.pallas_call(
        _kv_kernel,
        grid=(m // tm,),
        in_specs=[pl.BlockSpec((tm, d), lambda i: (i, 0)),
                  _const_spec((1, d)), _const_spec(w_c.shape), _const_spec(w_kr.shape),
                  _const_spec(w_kr_swapped.shape), _const_spec((1, KV_LORA)),
                  pl.BlockSpec((tm, ROPE_DIM), lambda i: (i % n_pos, 0)),
                  pl.BlockSpec((tm, ROPE_DIM), lambda i: (i % n_pos, 0))],
        out_specs=[pl.BlockSpec((tm, KV_LORA), lambda i: (i, 0)),
                   pl.BlockSpec((tm, ROPE_DIM), lambda i: (i, 0)),
                   pl.BlockSpec((tm, KV_LORA), lambda i: (i, 0)),
                   pl.BlockSpec((tm, ROPE_DIM), lambda i: (i, 0))],
        out_shape=[jax.ShapeDtypeStruct((m, KV_LORA), F32),
                   jax.ShapeDtypeStruct((m, ROPE_DIM), F32),
                   jax.ShapeDtypeStruct((m, KV_LORA), BF16),
                   jax.ShapeDtypeStruct((m, ROPE_DIM), BF16)],
        compiler_params=_params("parallel"),
        name="kv_shared",
    )(x, gain.reshape(1, d), w_c, w_kr, w_kr_swapped, gain_a.reshape(1, KV_LORA), cs, sn)


def _mla_q_kernel(x_ref, g_ref, wdq_ref, gq_ref, wn_ref, wr_ref, wrs_ref, wukt_ref, cs_ref, sn_ref,
                  ql_ref, qr_ref):
    h = _rms(x_ref[...], g_ref[...]).astype(BF16)
    qa = _rms(_dot(h, wdq_ref[...]), gq_ref[...]).astype(BF16)
    cs = cs_ref[...]
    sn = sn_ref[...]
    for hd in range(N_HEADS):
        q_nope = _dot(qa, wn_ref[hd]).astype(BF16)
        ql_ref[hd] = _dot(q_nope, wukt_ref[hd]).astype(BF16)
        qr_ref[hd] = (_dot(qa, wr_ref[hd]) * cs + _dot(qa, wrs_ref[hd]) * sn).astype(BF16)


def _mla_q(x, gain, w_dq, gain_q, w_nope, w_rope, w_rope_swapped, w_uk_t, cs, sn, tm):
    m, d = x.shape
    n_pos = cs.shape[0] // tm
    return pl.pallas_call(
        _mla_q_kernel,
        grid=(m // tm,),
        in_specs=[pl.BlockSpec((tm, d), lambda i: (i, 0)),
                  _const_spec((1, d)), _const_spec(w_dq.shape), _const_spec((1, Q_LORA)),
                  _const_spec(w_nope.shape), _const_spec(w_rope.shape),
                  _const_spec(w_rope_swapped.shape), _const_spec(w_uk_t.shape),
                  pl.BlockSpec((tm, ROPE_DIM), lambda i: (i % n_pos, 0)),
                  pl.BlockSpec((tm, ROPE_DIM), lambda i: (i % n_pos, 0))],
        out_specs=[pl.BlockSpec((N_HEADS, tm, KV_LORA), lambda i: (0, i, 0)),
                   pl.BlockSpec((N_HEADS, tm, ROPE_DIM), lambda i: (0, i, 0))],
        out_shape=[jax.ShapeDtypeStruct((N_HEADS, m, KV_LORA), BF16),
                   jax.ShapeDtypeStruct((N_HEADS, m, ROPE_DIM), BF16)],
        compiler_params=_params("parallel"),
        name="mla_q",
    )(x, gain.reshape(1, d), w_dq, gain_q.reshape(1, Q_LORA), w_nope, w_rope, w_rope_swapped, w_uk_t, cs, sn)


def _attn_prompt_kernel(ql_ref, qr_ref, c_ref, kr_ref, o_ref, m_scr, l_scr, acc_scr):
    i = pl.program_id(1)
    m_scr[...] = jnp.full_like(m_scr, MASK_VALUE)
    l_scr[...] = jnp.zeros_like(l_scr)
    acc_scr[...] = jnp.zeros_like(acc_scr)
    n_blocks = i // (ATTN_TK // ATTN_TQ) + 1

    def block(j, masked):
        start = pl.multiple_of(j * ATTN_TK, ATTN_TK)
        kc = c_ref[pl.ds(start, ATTN_TK), :]
        kr = kr_ref[pl.ds(start, ATTN_TK), :]
        if masked:
            q_pos = i * ATTN_TQ + lax.broadcasted_iota(jnp.int32, (ATTN_TQ, ATTN_TK), 0)
            k_pos = start + lax.broadcasted_iota(jnp.int32, (ATTN_TQ, ATTN_TK), 1)
            keep = k_pos <= q_pos
        for hd in range(N_HEADS):
            s = (_dot_nt(ql_ref[hd], kc) + _dot_nt(qr_ref[hd], kr)) * ATTN_SCALE
            if masked:
                s = jnp.where(keep, s, MASK_VALUE)
            m_prev = m_scr[hd]
            m_new = jnp.maximum(m_prev, jnp.max(s, axis=-1, keepdims=True))
            alpha = jnp.exp(m_prev - m_new)
            p = jnp.exp(s - m_new)
            l_scr[hd] = alpha * l_scr[hd] + jnp.sum(p, axis=-1, keepdims=True)
            acc_scr[hd] = alpha * acc_scr[hd] + _dot(p.astype(BF16), kc)
            m_scr[hd] = m_new

    def body(j, carry):
        block(j, False)
        return carry

    lax.fori_loop(0, n_blocks - 1, body, 0)
    block(n_blocks - 1, True)
    o_ref[...] = (acc_scr[...] / l_scr[...]).astype(BF16)


def _attn_prompt(ql, qr, cb, krb, batch, seq):
    nq = seq // ATTN_TQ
    return pl.pallas_call(
        _attn_prompt_kernel,
        grid=(batch, nq),
        in_specs=[pl.BlockSpec((N_HEADS, ATTN_TQ, KV_LORA), lambda b, i: (0, b * nq + i, 0)),
                  pl.BlockSpec((N_HEADS, ATTN_TQ, ROPE_DIM), lambda b, i: (0, b * nq + i, 0)),
                  pl.BlockSpec((seq, KV_LORA), lambda b, i: (b, 0)),
                  pl.BlockSpec((seq, ROPE_DIM), lambda b, i: (b, 0))],
        out_specs=pl.BlockSpec((N_HEADS, ATTN_TQ, KV_LORA), lambda b, i: (0, b * nq + i, 0)),
        out_shape=jax.ShapeDtypeStruct(ql.shape, BF16),
        scratch_shapes=[pltpu.VMEM((N_HEADS, ATTN_TQ, 1), F32),
                        pltpu.VMEM((N_HEADS, ATTN_TQ, 1), F32),
                        pltpu.VMEM((N_HEADS, ATTN_TQ, KV_LORA), F32)],
        compiler_params=_params("parallel", "arbitrary"),
        name="attn_prompt",
    )(ql, qr, cb, krb)


def _attn_sample_kernel(pt_ref, ql_ref, qr_ref, cn_ref, krn_ref, *rest):
    del pt_ref
    c_refs = rest[:PAGES_PER_STEP]
    kr_refs = rest[PAGES_PER_STEP:2 * PAGES_PER_STEP]
    o_ref, m_scr, l_scr, acc_scr = rest[2 * PAGES_PER_STEP:]
    j = pl.program_id(1)
    ql = ql_ref[0]
    qr = qr_ref[0]

    @pl.when(j == 0)
    def _():
        cn = cn_ref[0].astype(BF16).astype(F32)
        krn = krn_ref[0].astype(BF16).astype(F32)
        s0 = (jnp.sum(ql.astype(F32) * cn, axis=-1, keepdims=True)
              + jnp.sum(qr.astype(F32) * krn, axis=-1, keepdims=True)) * ATTN_SCALE
        m_scr[...] = s0
        l_scr[...] = jnp.ones_like(l_scr)
        acc_scr[...] = jnp.broadcast_to(cn, acc_scr.shape)

    scores = []
    keys = []
    for p in range(PAGES_PER_STEP):
        cb = c_refs[p][0].astype(BF16)
        kb = kr_refs[p][0].astype(BF16)
        scores.append((_dot_nt(ql, cb) + _dot_nt(qr, kb)) * ATTN_SCALE)
        keys.append(cb)
    m_prev = m_scr[...]
    m_new = m_prev
    for s in scores:
        m_new = jnp.maximum(m_new, jnp.max(s, axis=-1, keepdims=True))
    alpha = jnp.exp(m_prev - m_new)
    l_new = alpha * l_scr[...]
    acc = alpha * acc_scr[...]
    for s, cb in zip(scores, keys):
        p = jnp.exp(s - m_new)
        l_new = l_new + jnp.sum(p, axis=-1, keepdims=True)
        acc = acc + _dot(p.astype(BF16), cb)
    m_scr[...] = m_new
    l_scr[...] = l_new
    acc_scr[...] = acc

    @pl.when(j == pl.num_programs(1) - 1)
    def _():
        o_ref[0] = acc / l_new


def _attn_sample(ql, qr, c_new, kr_new, cache_c, cache_kr, page_table):
    n, n_pages = page_table.shape
    page = cache_c.shape[1]
    steps = n_pages // PAGES_PER_STEP

    def page_spec(width, p):
        return pl.BlockSpec((1, page, width), lambda b, j, pt: (pt[b, j * PAGES_PER_STEP + p], 0, 0))

    grid_spec = pltpu.PrefetchScalarGridSpec(
        num_scalar_prefetch=1,
        grid=(n, steps),
        in_specs=[pl.BlockSpec((1, N_HEADS, KV_LORA), lambda b, j, pt: (b, 0, 0)),
                  pl.BlockSpec((1, N_HEADS, ROPE_DIM), lambda b, j, pt: (b, 0, 0)),
                  pl.BlockSpec((1, 1, KV_LORA), lambda b, j, pt: (b, 0, 0)),
                  pl.BlockSpec((1, 1, ROPE_DIM), lambda b, j, pt: (b, 0, 0))]
                 + [page_spec(KV_LORA, p) for p in range(PAGES_PER_STEP)]
                 + [page_spec(ROPE_DIM, p) for p in range(PAGES_PER_STEP)],
        out_specs=pl.BlockSpec((1, N_HEADS, KV_LORA), lambda b, j, pt: (b, 0, 0)),
        scratch_shapes=[pltpu.VMEM((N_HEADS, 1), F32), pltpu.VMEM((N_HEADS, 1), F32),
                        pltpu.VMEM((N_HEADS, KV_LORA), F32)],
    )
    return pl.pallas_call(
        _attn_sample_kernel,
        grid_spec=grid_spec,
        out_shape=jax.ShapeDtypeStruct((n, N_HEADS, KV_LORA), F32),
        compiler_params=_params("parallel", "arbitrary"),
        name="attn_sample",
    )(page_table, ql, qr, c_new.reshape(n, 1, KV_LORA), kr_new.reshape(n, 1, ROPE_DIM),
      *([cache_c] * PAGES_PER_STEP), *([cache_kr] * PAGES_PER_STEP))


def _mla_out_kernel(ol_ref, wuv_ref, wout_ref, g_ref, x_ref, o_ref, o_scr):
    for hd in range(N_HEADS):
        o_scr[:, hd * HEAD_DIM:(hd + 1) * HEAD_DIM] = _dot(ol_ref[hd], wuv_ref[hd]).astype(BF16)
    y = _dot(o_scr[...], wout_ref[...])
    o_ref[...] = x_ref[...] + _rms(y, g_ref[...])


def _mla_out(o_lat, w_uv, w_out, gain, x, tm):
    m, d = x.shape
    return pl.pallas_call(
        _mla_out_kernel,
        grid=(m // tm,),
        in_specs=[pl.BlockSpec((N_HEADS, tm, KV_LORA), lambda i: (0, i, 0)),
                  _const_spec(w_uv.shape), _const_spec(w_out.shape), _const_spec((1, d)),
                  pl.BlockSpec((tm, d), lambda i: (i, 0))],
        out_specs=pl.BlockSpec((tm, d), lambda i: (i, 0)),
        out_shape=jax.ShapeDtypeStruct((m, d), F32),
        scratch_shapes=[pltpu.VMEM((tm, d), BF16)],
        compiler_params=_params("parallel"),
        name="mla_out",
    )(o_lat, w_uv, w_out, gain.reshape(1, d), x)


def _rope_tables(pos):
    half = ROPE_DIM // 2
    inv = ROPE_THETA ** (-jnp.arange(half, dtype=F32) / half)
    ang = pos.astype(F32)[:, None] * inv[None, :]
    cos, sin = jnp.cos(ang), jnp.sin(ang)
    return jnp.concatenate([cos, cos], axis=-1), jnp.concatenate([-sin, sin], axis=-1)


def _swap_halves(w):
    half = w.shape[-1] // 2
    return jnp.concatenate([w[..., half:], w[..., :half]], axis=-1)


def kernel(x_prompt, x_sample, state_hgrn, cache_kv_latent, cache_k_rope, page_table, norm_gains, w_ffn_in, w_ffn_out, w_in_a, lb_logits, g_norm_a, w_out_a, kv_norm, w_dkv, kv_a_norm, w_ukv, w_dq, q_a_norm, w_uq, w_out_b):
    batch, seq, d = x_prompt.shape
    n_s = x_sample.shape[0]
    depth = norm_gains.shape[0]
    n_hgrn = w_in_a.shape[0]
    past_len = page_table.shape[1] * cache_kv_latent.shape[1]
    tm_p, tm_s = 512, n_s

    w_ffn_in_b = w_ffn_in.astype(BF16)
    w_ffn_out_b = w_ffn_out.astype(BF16)
    w_in_a_b = w_in_a.astype(BF16)
    w_out_a_b = w_out_a.astype(BF16)
    w_out_b_b = w_out_b.astype(BF16)
    w_dq_b = w_dq.astype(BF16)
    w_c = w_dkv[:, :KV_LORA].astype(BF16)
    w_kr = w_dkv[:, KV_LORA:].astype(BF16)
    w_kr_sw = _swap_halves(w_kr)
    w_ukv_h = w_ukv.reshape(KV_LORA, N_HEADS, 2 * HEAD_DIM).astype(BF16)
    w_uk_t = w_ukv_h[:, :, :HEAD_DIM].transpose(1, 2, 0)
    w_uv = w_ukv_h[:, :, HEAD_DIM:].transpose(1, 0, 2)
    w_uq_h = w_uq.reshape(-1, Q_LORA, N_HEADS, HEAD_DIM + ROPE_DIM).astype(BF16)
    w_q_nope = w_uq_h[..., :HEAD_DIM].transpose(0, 2, 1, 3)
    w_q_rope = w_uq_h[..., HEAD_DIM:].transpose(0, 2, 1, 3)
    w_q_rope_sw = _swap_halves(w_q_rope)

    cs_p, sn_p = _rope_tables(jnp.arange(seq, dtype=jnp.int32))
    cs_s, sn_s = _rope_tables(jnp.full((n_s,), past_len, dtype=jnp.int32))

    xp = x_prompt.reshape(batch * seq, d)
    xs = x_sample.reshape(n_s, d)
    states_p, states_s = [], []
    kv_p = kv_s = None
    for l in range(depth):
        g = norm_gains[l]
        if l < n_hgrn:
            zp = _norm_mm(xp, g[0], w_in_a_b[l], tm_p)
            zs = _norm_mm(xs, g[0], w_in_a_b[l], tm_s)
            op, st_p = _hgrn_prompt(zp, lb_logits, g_norm_a[l], l, batch, seq)
            os_, st_s = _hgrn_sample(zs, state_hgrn, lb_logits, g_norm_a[l], l)
            states_p.append(st_p)
            states_s.append(st_s)
            xp = _mm_post(op, w_out_a_b[l], g[1], xp, tm_p)
            xs = _mm_post(os_.astype(BF16), w_out_a_b[l], g[1], xs, tm_s)
        else:
            j = l - n_hgrn
            if kv_p is None:
                kv_p = _kv_shared(xp, kv_norm, w_c, w_kr, w_kr_sw, kv_a_norm, cs_p, sn_p, tm_p)
                kv_s = _kv_shared(xs, kv_norm, w_c, w_kr, w_kr_sw, kv_a_norm, cs_s, sn_s, tm_s)
            q_args = (g[0], w_dq_b[j], q_a_norm[j], w_q_nope[j], w_q_rope[j], w_q_rope_sw[j], w_uk_t)
            ql_p, qr_p = _mla_q(xp, *q_args, cs_p, sn_p, tm_p)
            ql_s, qr_s = _mla_q(xs, *q_args, cs_s, sn_s, tm_s)
            ol_p = _attn_prompt(ql_p, qr_p, kv_p[2], kv_p[3], batch, seq)
            ol_s = _attn_sample(ql_s.transpose(1, 0, 2), qr_s.transpose(1, 0, 2), kv_s[0], kv_s[1],
                                cache_kv_latent, cache_k_rope, page_table)
            xp = _mla_out(ol_p, w_uv, w_out_b_b[j], g[1], xp, tm_p)
            xs = _mla_out(ol_s.astype(BF16).transpose(1, 0, 2), w_uv, w_out_b_b[j], g[1], xs, tm_s)
        xp = _ffn(xp, g[2], g[3], w_ffn_in_b[l], w_ffn_out_b[l], tm_p)
        xs = _ffn(xs, g[2], g[3], w_ffn_in_b[l], w_ffn_out_b[l], tm_s)

    return (xp.reshape(batch, seq, d), xs.reshape(n_s, 1, d),
            jnp.stack(states_p), kv_p[0].reshape(batch, seq, KV_LORA), kv_p[1].reshape(batch, seq, ROPE_DIM),
            jnp.stack(states_s), kv_s[0].reshape(n_s, 1, KV_LORA), kv_s[1].reshape(n_s, 1, ROPE_DIM))
```

```python
import functools

import jax
import jax.numpy as jnp
from jax import lax
from jax.experimental import pallas as pl
from jax.experimental.pallas import tpu as pltpu

F32 = jnp.float32
BF16 = jnp.bfloat16

D_MODEL = 1024
N_HEADS = 8
HEAD_DIM = 128
ROPE_DIM = 64
Q_LORA = 384
KV_LORA = 256
D_FF = 2816
N_HGRN = 2
EPS = 1e-6
LB_FLOOR = 1e-30
ROPE_THETA = 10000.0
ATTN_SCALE = (HEAD_DIM + ROPE_DIM) ** -0.5
MASK_VALUE = -1e30

HGRN_CHUNK = 128
FFN_COLS = 256
ATTN_TQ = 128
ATTN_TK = 256
PAGES_PER_STEP = 32
VMEM_LIMIT = 56 * 1024 * 1024
LANES = 128

_NT = (((1,), (1,)), ((), ()))
_TN = (((0,), (0,)), ((), ()))


def _params(*sem):
    return pltpu.CompilerParams(dimension_semantics=sem, vmem_limit_bytes=VMEM_LIMIT)


def _dot(a, b):
    return jnp.dot(a, b, preferred_element_type=F32)


def _dot_nt(a, b):
    return lax.dot_general(a, b, _NT, preferred_element_type=F32)


def _rms(xf, gain):
    ms = jnp.mean(xf * xf, axis=-1, keepdims=True)
    return xf * lax.rsqrt(ms + EPS) * gain


def _silu(x):
    return x * jax.nn.sigmoid(x)


def _const_spec(shape):
    n = len(shape)
    return pl.BlockSpec(shape, lambda *_: (0,) * n)


def _norm_mm_kernel(x_ref, g_ref, w_ref, o_ref, *, cols):
    h = _rms(x_ref[...], g_ref[...]).astype(BF16)
    for c in range(0, o_ref.shape[1], cols):
        o_ref[:, c:c + cols] = _dot(h, w_ref[:, c:c + cols])


def _norm_mm(x, gain, w, tm):
    m, d = x.shape
    n = w.shape[1]
    return pl.pallas_call(
        functools.partial(_norm_mm_kernel, cols=1024),
        grid=(m // tm,),
        in_specs=[pl.BlockSpec((tm, d), lambda i: (i, 0)),
                  _const_spec((1, d)), _const_spec((d, n))],
        out_specs=pl.BlockSpec((tm, n), lambda i: (i, 0)),
        out_shape=jax.ShapeDtypeStruct((m, n), F32),
        compiler_params=_params("parallel"),
        name="norm_mm",
    )(x, gain.reshape(1, d), w)


def _mm_post_kernel(a_ref, w_ref, g_ref, x_ref, o_ref):
    y = _dot(a_ref[...], w_ref[...])
    o_ref[...] = x_ref[...] + _rms(y, g_ref[...])


def _mm_post(a, w, gain, x, tm):
    m, k = a.shape
    d = w.shape[1]
    return pl.pallas_call(
        _mm_post_kernel,
        grid=(m // tm,),
        in_specs=[pl.BlockSpec((tm, k), lambda i: (i, 0)),
                  _const_spec((k, d)), _const_spec((1, d)),
                  pl.BlockSpec((tm, d), lambda i: (i, 0))],
        out_specs=pl.BlockSpec((tm, d), lambda i: (i, 0)),
        out_shape=jax.ShapeDtypeStruct((m, d), F32),
        compiler_params=_params("parallel"),
        name="mm_post",
    )(a, w, gain.reshape(1, d), x)


def _ffn_kernel(x_ref, g_in_ref, g_out_ref, w_in_ref, w_out_ref, o_ref, a_scr):
    x = x_ref[...]
    h = _rms(x, g_in_ref[...]).astype(BF16)
    for c in range(0, D_FF, FFN_COLS):
        g = _dot(h, w_in_ref[:, c:c + FFN_COLS])
        u = _dot(h, w_in_ref[:, D_FF + c:D_FF + c + FFN_COLS])
        a_scr[:, c:c + FFN_COLS] = (_silu(g) * u).astype(BF16)
    y = _dot(a_scr[...], w_out_ref[...])
    o_ref[...] = x + _rms(y, g_out_ref[...])


def _ffn(x, g_in, g_out, w_in, w_out, tm):
    m, d = x.shape
    return pl.pallas_call(
        _ffn_kernel,
        grid=(m // tm,),
        in_specs=[pl.BlockSpec((tm, d), lambda i: (i, 0)),
                  _const_spec((1, d)), _const_spec((1, d)),
                  _const_spec((d, 2 * D_FF)), _const_spec((D_FF, d))],
        out_specs=pl.BlockSpec((tm, d), lambda i: (i, 0)),
        out_shape=jax.ShapeDtypeStruct((m, d), F32),
        scratch_shapes=[pltpu.VMEM((tm, D_FF), BF16)],
        compiler_params=_params("parallel"),
        name="ffn",
    )(x, g_in.reshape(1, d), g_out.reshape(1, d), w_in, w_out)


def _lower_bound(logits, layer):
    m = jnp.max(logits, axis=0, keepdims=True)
    e = jnp.exp(logits - m)
    p = e / jnp.sum(e, axis=0, keepdims=True)
    acc = p[0:1]
    for r in range(1, layer + 1):
        acc = acc + p[r:r + 1]
    return acc - p[0:1]


def _forget_gate(fz, lb):
    f = jnp.maximum(lb, LB_FLOOR) + (1.0 - lb) * jax.nn.sigmoid(fz)
    return jnp.minimum(f, 1.0)


def _mid_rows(b, m):
    c = b.shape[0]
    if m >= 4:
        g3 = b.reshape(c // (2 * m), 2 * m, b.shape[1])
        return jnp.broadcast_to(g3[:, m - 1:m, :], g3.shape).reshape(b.shape)
    g3 = b.reshape(c // 8, 8, b.shape[1])
    sub = lax.broadcasted_iota(jnp.int32, g3.shape, 1)

    def row(r):
        return jnp.broadcast_to(g3[:, r:r + 1, :], g3.shape)

    if m == 2:
        out = jnp.where(sub < 4, row(1), row(5))
    else:
        out = jnp.where(sub < 2, row(0), jnp.where(sub < 4, row(2), jnp.where(sub < 6, row(4), row(6))))
    return out.reshape(b.shape)


def _hgrn_prompt_kernel(z_ref, lbl_ref, gn_ref, o_ref, st_ref, state_scr, *, layer):
    c = pl.program_id(1)
    chunk = z_ref.shape[0]

    @pl.when(c == 0)
    def _():
        state_scr[...] = jnp.zeros_like(state_scr)

    lb_all = _lower_bound(lbl_ref[...], layer)
    ti = lax.broadcasted_iota(jnp.int32, (chunk, chunk), 0)
    si = lax.broadcasted_iota(jnp.int32, (chunk, chunk), 1)
    level = jnp.where(ti > si, ti ^ si, 0)
    eye = ti == si
    tril = (ti >= si).astype(F32)
    row = lax.broadcasted_iota(jnp.int32, (chunk, HEAD_DIM), 0)
    signs = []
    m = 1
    while m < chunk:
        signs.append((m, jnp.where((row & m) != 0, 1.0, -1.0)))
        m *= 2

    for h in range(N_HEADS):
        lo = h * HEAD_DIM
        hi = lo + HEAD_DIM
        qz = z_ref[:, lo:hi]
        fz = z_ref[:, D_MODEL + lo:D_MODEL + hi]
        v = z_ref[:, 2 * D_MODEL + lo:2 * D_MODEL + hi]
        gz = z_ref[:, 3 * D_MODEL + lo:3 * D_MODEL + hi]

        f = _forget_gate(fz, lb_all[:, lo:hi])
        k = 1.0 - f
        q = _silu(qz) * (HEAD_DIM ** -0.5)
        vb = v.astype(BF16)
        b = jnp.dot(tril, jnp.log2(f), precision=lax.Precision.HIGHEST, preferred_element_type=F32)

        a = jnp.where(eye, _dot_nt(q.astype(BF16), k.astype(BF16)), 0.0)
        for m, sign in signs:
            e = jnp.exp2((b - _mid_rows(b, m)) * sign)
            a = jnp.where(level >= m, _dot_nt((q * e).astype(BF16), (k * e).astype(BF16)), a)

        st = state_scr[h]
        o = _dot(a.astype(BF16), vb) + _dot_nt((q * jnp.exp2(b)).astype(BF16), st.astype(BF16))
        b_last = b[chunk - 1:chunk, :]
        kd = (k * jnp.exp2(b_last - b)).astype(BF16)
        state_scr[h] = jnp.exp2(b_last) * st + lax.dot_general(vb, kd, _TN, preferred_element_type=F32)

        o = o * lax.rsqrt(jnp.mean(o * o, axis=-1, keepdims=True) + EPS)
        o_ref[:, lo:hi] = (o * gn_ref[:, lo:hi] * _silu(gz)).astype(BF16)

    @pl.when(c == pl.num_programs(1) - 1)
    def _():
        for h in range(N_HEADS):
            st_ref[0, h] = state_scr[h].T


def _hgrn_prompt(z, lb_logits, g_norm, layer, batch, seq):
    n_chunks = seq // HGRN_CHUNK
    return pl.pallas_call(
        functools.partial(_hgrn_prompt_kernel, layer=layer),
        grid=(batch, n_chunks),
        in_specs=[pl.BlockSpec((HGRN_CHUNK, 4 * D_MODEL), lambda b, c: (b * n_chunks + c, 0)),
                  _const_spec(lb_logits.shape), _const_spec((1, D_MODEL))],
        out_specs=[pl.BlockSpec((HGRN_CHUNK, D_MODEL), lambda b, c: (b * n_chunks + c, 0)),
                   pl.BlockSpec((1, N_HEADS, HEAD_DIM, HEAD_DIM), lambda b, c: (b, 0, 0, 0))],
        out_shape=[jax.ShapeDtypeStruct((batch * seq, D_MODEL), BF16),
                   jax.ShapeDtypeStruct((batch, N_HEADS, HEAD_DIM, HEAD_DIM), F32)],
        scratch_shapes=[pltpu.VMEM((N_HEADS, HEAD_DIM, HEAD_DIM), F32)],
        compiler_params=_params("parallel", "arbitrary"),
        name="hgrn_prompt",
    )(z, lb_logits, g_norm.reshape(1, D_MODEL))


SAMPLE_GROUP = 4


def _hgrn_sample_kernel(*refs, layers, write_states):
    n = len(layers)
    z_refs = refs[:n]
    s_ref, lbl_ref, gn_ref, o_ref = refs[n:n + 4]
    so_ref = refs[n + 4] if write_states else None
    gn = gn_ref[...]
    for li, layer in enumerate(layers):
        z_ref = z_refs[li]
        lb = _lower_bound(lbl_ref[...], layer).reshape(N_HEADS, HEAD_DIM)
        z3 = z_ref[...].reshape(SAMPLE_GROUP * 4, N_HEADS, HEAD_DIM)
        part = lax.broadcasted_iota(jnp.int32, z3.shape, 0) % 4
        cols3 = jnp.where(part == 0, _silu(z3) * (HEAD_DIM ** -0.5),
                          jnp.where(part == 1, _forget_gate(z3, lb[None]), 0.0))
        cols = cols3.reshape(SAMPLE_GROUP * 4 * N_HEADS, HEAD_DIM).T
        for i in range(SAMPLE_GROUP):
            for h in range(N_HEADS):
                base = i * 4 * N_HEADS + h
                f_col = cols[:, base + N_HEADS:base + N_HEADS + 1]
                v_row = z_ref[i, 2 * N_HEADS + h:2 * N_HEADS + h + 1, :]
                s_new = f_col * s_ref[li, i, h] + (1.0 - f_col) * v_row
                if write_states:
                    so_ref[li, i, h] = s_new
                if li == n - 1:
                    q_col = cols[:, base:base + 1]
                    gz_row = z_ref[i, 3 * N_HEADS + h:3 * N_HEADS + h + 1, :]
                    o = jnp.sum(q_col * s_new, axis=0, keepdims=True)
                    o = o * lax.rsqrt(jnp.mean(o * o, axis=-1, keepdims=True) + EPS)
                    o_ref[i, h:h + 1, :] = o * gn[h:h + 1, :] * _silu(gz_row)


def _hgrn_sample(zs, state, lb_logits, g_norm, layer, write_states):
    n = zs[0].shape[0]
    layers = tuple(range(layer + 1)) if write_states else (layer,)
    assert len(zs) == len(layers)
    state_block = (len(layers), SAMPLE_GROUP, N_HEADS, HEAD_DIM, HEAD_DIM)
    first = layers[0]
    out_specs = [pl.BlockSpec((SAMPLE_GROUP, N_HEADS, HEAD_DIM), lambda i: (i, 0, 0))]
    out_shape = [jax.ShapeDtypeStruct((n, N_HEADS, HEAD_DIM), F32)]
    if write_states:
        out_specs.append(pl.BlockSpec(state_block, lambda i: (0, i, 0, 0, 0)))
        out_shape.append(jax.ShapeDtypeStruct((len(layers), n, N_HEADS, HEAD_DIM, HEAD_DIM), F32))
    outs = pl.pallas_call(
        functools.partial(_hgrn_sample_kernel, layers=layers, write_states=write_states),
        grid=(n // SAMPLE_GROUP,),
        in_specs=[pl.BlockSpec((SAMPLE_GROUP, 4 * N_HEADS, HEAD_DIM), lambda i: (i, 0, 0)) for _ in zs]
                 + [pl.BlockSpec(state_block, lambda i: (first, i, 0, 0, 0)),
                    _const_spec(lb_logits.shape), _const_spec((N_HEADS, HEAD_DIM))],
        out_specs=out_specs,
        out_shape=out_shape,
        compiler_params=_params("parallel"),
        name="hgrn_sample",
    )(*[z.reshape(n, 4 * N_HEADS, HEAD_DIM) for z in zs], state, lb_logits, g_norm.reshape(N_HEADS, HEAD_DIM))
    o = outs[0].reshape(n, D_MODEL)
    return (o, outs[1]) if write_states else (o, None)


def _kv_kernel(x_ref, g_ref, wc_ref, wkr_ref, wkrs_ref, ga_ref, cs_ref, sn_ref,
               c_ref, kr_ref, cb_ref, krb_ref):
    h = _rms(x_ref[...], g_ref[...]).astype(BF16)
    c = _rms(_dot(h, wc_ref[...]), ga_ref[...])
    kr = _dot(h, wkr_ref[...]) * cs_ref[...] + _dot(h, wkrs_ref[...]) * sn_ref[...]
    c_ref[...] = c
    kr_ref[...] = kr
    cb_ref[...] = c.astype(BF16)
    krb_ref[...] = kr.astype(BF16)


def _kv_shared(x, gain, w_c, w_kr, w_kr_swapped, gain_a, cs, sn, tm):
    m, d = x.shape
    n_pos = cs.shape[0] // tm
    return pl.pallas_call(
        _kv_kernel,
        grid=(m // tm,),
        in_specs=[pl.BlockSpec((tm, d), lambda i: (i, 0)),
                  _const_spec((1, d)), _const_spec(w_c.shape), _const_spec(w_kr.shape),
                  _const_spec(w_kr_swapped.shape), _const_spec((1, KV_LORA)),
                  pl.BlockSpec((tm, ROPE_DIM), lambda i: (i % n_pos, 0)),
                  pl.BlockSpec((tm, ROPE_DIM), lambda i: (i % n_pos, 0))],
        out_specs=[pl.BlockSpec((tm, KV_LORA), lambda i: (i, 0)),
                   pl.BlockSpec((tm, ROPE_DIM), lambda i: (i, 0)),
                   pl.BlockSpec((tm, KV_LORA), lambda i: (i, 0)),
                   pl.BlockSpec((tm, ROPE_DIM), lambda i: (i, 0))],
        out_shape=[jax.ShapeDtypeStruct((m, KV_LORA), F32),
                   jax.ShapeDtypeStruct((m, ROPE_DIM), F32),
                   jax.ShapeDtypeStruct((m, KV_LORA), BF16),
                   jax.ShapeDtypeStruct((m, ROPE_DIM), BF16)],
        compiler_params=_params("parallel"),
        name="kv_shared",
    )(x, gain.reshape(1, d), w_c, w_kr, w_kr_swapped, gain_a.reshape(1, KV_LORA), cs, sn)


def _mla_q_kernel(x_ref, g_ref, wdq_ref, gq_ref, wn_ref, wr_ref, wrs_ref, wukt_ref, cs_ref, sn_ref,
                  ql_ref, qr_ref):
    h = _rms(x_ref[...], g_ref[...]).astype(BF16)
    qa = _rms(_dot(h, wdq_ref[...]), gq_ref[...]).astype(BF16)
    cs = cs_ref[...]
    sn = sn_ref[...]
    for hd in range(N_HEADS):
        q_nope = _dot(qa, wn_ref[hd]).astype(BF16)
        ql_ref[hd] = _dot(q_nope, wukt_ref[hd]).astype(BF16)
        qr_ref[hd] = (_dot(qa, wr_ref[hd]) * cs + _dot(qa, wrs_ref[hd]) * sn).astype(BF16)


def _mla_q(x, gain, w_dq, gain_q, w_nope, w_rope, w_rope_swapped, w_uk_t, cs, sn, tm):
    m, d = x.shape
    n_pos = cs.shape[0] // tm
    return pl.pallas_call(
        _mla_q_kernel,
        grid=(m // tm,),
        in_specs=[pl.BlockSpec((tm, d), lambda i: (i, 0)),
                  _const_spec((1, d)), _const_spec(w_dq.shape), _const_spec((1, Q_LORA)),
                  _const_spec(w_nope.shape), _const_spec(w_rope.shape),
                  _const_spec(w_rope_swapped.shape), _const_spec(w_uk_t.shape),
                  pl.BlockSpec((tm, ROPE_DIM), lambda i: (i % n_pos, 0)),
                  pl.BlockSpec((tm, ROPE_DIM), lambda i: (i % n_pos, 0))],
        out_specs=[pl.BlockSpec((N_HEADS, tm, KV_LORA), lambda i: (0, i, 0)),
                   pl.BlockSpec((N_HEADS, tm, ROPE_DIM), lambda i: (0, i, 0))],
        out_shape=[jax.ShapeDtypeStruct((N_HEADS, m, KV_LORA), BF16),
                   jax.ShapeDtypeStruct((N_HEADS, m, ROPE_DIM), BF16)],
        compiler_params=_params("parallel"),
        name="mla_q",
    )(x, gain.reshape(1, d), w_dq, gain_q.reshape(1, Q_LORA), w_nope, w_rope, w_rope_swapped, w_uk_t, cs, sn)


def _attn_prompt_kernel(ql_ref, qr_ref, c_ref, kr_ref, o_ref, m_scr, l_scr, acc_scr):
    i = pl.program_id(1)
    rows = N_HEADS * ATTN_TQ
    m_scr[...] = jnp.full_like(m_scr, MASK_VALUE)
    l_scr[...] = jnp.zeros_like(l_scr)
    acc_scr[...] = jnp.zeros_like(acc_scr)
    ql = ql_ref[...].reshape(rows, KV_LORA)
    qr = qr_ref[...].reshape(rows, ROPE_DIM)
    n_blocks = i // (ATTN_TK // ATTN_TQ) + 1

    def block(j, masked):
        start = pl.multiple_of(j * ATTN_TK, ATTN_TK)
        kc = c_ref[pl.ds(start, ATTN_TK), :]
        kr = kr_ref[pl.ds(start, ATTN_TK), :]
        s = _dot_nt(ql, kc) + _dot_nt(qr, kr)
        if masked:
            shape = (N_HEADS, ATTN_TQ, ATTN_TK)
            q_pos = i * ATTN_TQ + lax.broadcasted_iota(jnp.int32, shape, 1).reshape(rows, ATTN_TK)
            k_pos = start + lax.broadcasted_iota(jnp.int32, (rows, ATTN_TK), 1)
            s = jnp.where(k_pos <= q_pos, s, MASK_VALUE)
        tiles = [s[:, t:t + LANES] for t in range(0, ATTN_TK, LANES)]
        tile_max = functools.reduce(jnp.maximum, tiles)
        m_prev = m_scr[...]
        m_new = jnp.maximum(m_prev, jnp.max(tile_max, axis=-1, keepdims=True))
        alpha = jnp.exp((m_prev - m_new) * ATTN_SCALE)
        ps = [jnp.exp((t - m_new) * ATTN_SCALE) for t in tiles]
        l_scr[...] = alpha * l_scr[...] + functools.reduce(jnp.add, ps)
        pv = _dot(jnp.concatenate(ps, axis=1).astype(BF16), kc)
        for t in range(0, KV_LORA, LANES):
            acc_scr[:, t:t + LANES] = alpha * acc_scr[:, t:t + LANES] + pv[:, t:t + LANES]
        m_scr[...] = m_new

    def body(j, carry):
        block(j, False)
        return carry

    lax.fori_loop(0, n_blocks - 1, body, 0)
    block(n_blocks - 1, True)
    inv_l = 1.0 / jnp.sum(l_scr[...], axis=-1, keepdims=True)
    o_ref[...] = (acc_scr[...] * inv_l).reshape(o_ref.shape).astype(BF16)


def _attn_prompt(ql, qr, cb, krb, batch, seq):
    nq = seq // ATTN_TQ
    return pl.pallas_call(
        _attn_prompt_kernel,
        grid=(batch, nq),
        in_specs=[pl.BlockSpec((N_HEADS, ATTN_TQ, KV_LORA), lambda b, i: (0, b * nq + i, 0)),
                  pl.BlockSpec((N_HEADS, ATTN_TQ, ROPE_DIM), lambda b, i: (0, b * nq + i, 0)),
                  pl.BlockSpec((seq, KV_LORA), lambda b, i: (b, 0)),
                  pl.BlockSpec((seq, ROPE_DIM), lambda b, i: (b, 0))],
        out_specs=pl.BlockSpec((N_HEADS, ATTN_TQ, KV_LORA), lambda b, i: (0, b * nq + i, 0)),
        out_shape=jax.ShapeDtypeStruct(ql.shape, BF16),
        scratch_shapes=[pltpu.VMEM((N_HEADS * ATTN_TQ, LANES), F32),
                        pltpu.VMEM((N_HEADS * ATTN_TQ, LANES), F32),
                        pltpu.VMEM((N_HEADS * ATTN_TQ, KV_LORA), F32)],
        compiler_params=_params("parallel", "arbitrary"),
        name="attn_prompt",
    )(ql, qr, cb, krb)


def _attn_sample_kernel(pt_ref, ql_ref, qr_ref, cn_ref, krn_ref, *rest):
    del pt_ref
    c_refs = rest[:PAGES_PER_STEP]
    kr_refs = rest[PAGES_PER_STEP:2 * PAGES_PER_STEP]
    o_ref, m_scr, l_scr, acc_scr = rest[2 * PAGES_PER_STEP:]
    j = pl.program_id(1)
    ql = ql_ref[0]
    qr = qr_ref[0]

    @pl.when(j == 0)
    def _():
        cn = cn_ref[0].astype(BF16).astype(F32)
        krn = krn_ref[0].astype(BF16).astype(F32)
        m_scr[...] = (jnp.sum(ql.astype(F32) * cn, axis=-1, keepdims=True)
                      + jnp.sum(qr.astype(F32) * krn, axis=-1, keepdims=True))
        l_scr[...] = jnp.ones_like(l_scr)
        acc_scr[...] = jnp.broadcast_to(cn, acc_scr.shape)

    kc = jnp.concatenate([r[0].astype(BF16) for r in c_refs], axis=0)
    kr = jnp.concatenate([r[0].astype(BF16) for r in kr_refs], axis=0)
    s = _dot_nt(ql, kc) + _dot_nt(qr, kr)
    m_prev = m_scr[...]
    m_new = jnp.maximum(m_prev, jnp.max(s, axis=-1, keepdims=True))
    alpha = jnp.exp((m_prev - m_new) * ATTN_SCALE)
    p = jnp.exp((s - m_new) * ATTN_SCALE)
    l_new = alpha * l_scr[...] + jnp.sum(p, axis=-1, keepdims=True)
    acc = alpha * acc_scr[...] + _dot(p.astype(BF16), kc)
    m_scr[...] = m_new
    l_scr[...] = l_new
    acc_scr[...] = acc

    @pl.when(j == pl.num_programs(1) - 1)
    def _():
        o_ref[0] = acc / l_new


def _attn_sample(ql, qr, c_new, kr_new, cache_c, cache_kr, page_table):
    n, n_pages = page_table.shape
    page = cache_c.shape[1]
    steps = n_pages // PAGES_PER_STEP

    def page_spec(width, p):
        return pl.BlockSpec((1, page, width), lambda b, j, pt: (pt[b, j * PAGES_PER_STEP + p], 0, 0))

    grid_spec = pltpu.PrefetchScalarGridSpec(
        num_scalar_prefetch=1,
        grid=(n, steps),
        in_specs=[pl.BlockSpec((1, N_HEADS, KV_LORA), lambda b, j, pt: (b, 0, 0)),
                  pl.BlockSpec((1, N_HEADS, ROPE_DIM), lambda b, j, pt: (b, 0, 0)),
                  pl.BlockSpec((1, 1, KV_LORA), lambda b, j, pt: (b, 0, 0)),
                  pl.BlockSpec((1, 1, ROPE_DIM), lambda b, j, pt: (b, 0, 0))]
                 + [page_spec(KV_LORA, p) for p in range(PAGES_PER_STEP)]
                 + [page_spec(ROPE_DIM, p) for p in range(PAGES_PER_STEP)],
        out_specs=pl.BlockSpec((1, N_HEADS, KV_LORA), lambda b, j, pt: (b, 0, 0)),
        scratch_shapes=[pltpu.VMEM((N_HEADS, 1), F32), pltpu.VMEM((N_HEADS, 1), F32),
                        pltpu.VMEM((N_HEADS, KV_LORA), F32)],
    )
    return pl.pallas_call(
        _attn_sample_kernel,
        grid_spec=grid_spec,
        out_shape=jax.ShapeDtypeStruct((n, N_HEADS, KV_LORA), F32),
        compiler_params=_params("parallel", "arbitrary"),
        name="attn_sample",
    )(page_table, ql, qr, c_new.reshape(n, 1, KV_LORA), kr_new.reshape(n, 1, ROPE_DIM),
      *([cache_c] * PAGES_PER_STEP), *([cache_kr] * PAGES_PER_STEP))


def _mla_out_kernel(ol_ref, wuv_ref, wout_ref, g_ref, x_ref, o_ref, o_scr):
    for hd in range(N_HEADS):
        o_scr[:, hd * HEAD_DIM:(hd + 1) * HEAD_DIM] = _dot(ol_ref[hd], wuv_ref[hd]).astype(BF16)
    y = _dot(o_scr[...], wout_ref[...])
    o_ref[...] = x_ref[...] + _rms(y, g_ref[...])


def _mla_out(o_lat, w_uv, w_out, gain, x, tm):
    m, d = x.shape
    return pl.pallas_call(
        _mla_out_kernel,
        grid=(m // tm,),
        in_specs=[pl.BlockSpec((N_HEADS, tm, KV_LORA), lambda i: (0, i, 0)),
                  _const_spec(w_uv.shape), _const_spec(w_out.shape), _const_spec((1, d)),
                  pl.BlockSpec((tm, d), lambda i: (i, 0))],
        out_specs=pl.BlockSpec((tm, d), lambda i: (i, 0)),
        out_shape=jax.ShapeDtypeStruct((m, d), F32),
        scratch_shapes=[pltpu.VMEM((tm, d), BF16)],
        compiler_params=_params("parallel"),
        name="mla_out",
    )(o_lat, w_uv, w_out, gain.reshape(1, d), x)


def _rope_tables(pos):
    half = ROPE_DIM // 2
    inv = ROPE_THETA ** (-jnp.arange(half, dtype=F32) / half)
    ang = pos.astype(F32)[:, None] * inv[None, :]
    cos, sin = jnp.cos(ang), jnp.sin(ang)
    return jnp.concatenate([cos, cos], axis=-1), jnp.concatenate([-sin, sin], axis=-1)


def _swap_halves(w):
    half = w.shape[-1] // 2
    return jnp.concatenate([w[..., half:], w[..., :half]], axis=-1)


def kernel(x_prompt, x_sample, state_hgrn, cache_kv_latent, cache_k_rope, page_table, norm_gains, w_ffn_in, w_ffn_out, w_in_a, lb_logits, g_norm_a, w_out_a, kv_norm, w_dkv, kv_a_norm, w_ukv, w_dq, q_a_norm, w_uq, w_out_b):
    batch, seq, d = x_prompt.shape
    n_s = x_sample.shape[0]
    depth = norm_gains.shape[0]
    n_hgrn = w_in_a.shape[0]
    past_len = page_table.shape[1] * cache_kv_latent.shape[1]
    tm_p, tm_s = 512, n_s

    w_ffn_in_b = w_ffn_in.astype(BF16)
    w_ffn_out_b = w_ffn_out.astype(BF16)
    w_in_a_b = w_in_a.astype(BF16)
    w_out_a_b = w_out_a.astype(BF16)
    w_out_b_b = w_out_b.astype(BF16)
    w_dq_b = w_dq.astype(BF16)
    w_c = w_dkv[:, :KV_LORA].astype(BF16)
    w_kr = w_dkv[:, KV_LORA:].astype(BF16)
    w_kr_sw = _swap_halves(w_kr)
    w_ukv_h = w_ukv.reshape(KV_LORA, N_HEADS, 2 * HEAD_DIM).astype(BF16)
    w_uk_t = w_ukv_h[:, :, :HEAD_DIM].transpose(1, 2, 0)
    w_uv = w_ukv_h[:, :, HEAD_DIM:].transpose(1, 0, 2)
    w_uq_h = w_uq.reshape(-1, Q_LORA, N_HEADS, HEAD_DIM + ROPE_DIM).astype(BF16)
    w_q_nope = w_uq_h[..., :HEAD_DIM].transpose(0, 2, 1, 3)
    w_q_rope = w_uq_h[..., HEAD_DIM:].transpose(0, 2, 1, 3)
    w_q_rope_sw = _swap_halves(w_q_rope)

    cs_p, sn_p = _rope_tables(jnp.arange(seq, dtype=jnp.int32))
    cs_s, sn_s = _rope_tables(jnp.full((n_s,), past_len, dtype=jnp.int32))

    xp = x_prompt.reshape(batch * seq, d)
    xs = x_sample.reshape(n_s, d)
    states_p, zs_all, states_s = [], [], None
    kv_p = kv_s = None
    for l in range(depth):
        g = norm_gains[l]
        if l < n_hgrn:
            zp = _norm_mm(xp, g[0], w_in_a_b[l], tm_p)
            zs = _norm_mm(xs, g[0], w_in_a_b[l], tm_s)
            op, st_p = _hgrn_prompt(zp, lb_logits, g_norm_a[l], l, batch, seq)
            zs_all.append(zs)
            last = l == n_hgrn - 1
            os_, states_s = _hgrn_sample(zs_all if last else [zs], state_hgrn, lb_logits, g_norm_a[l], l, last)
            states_p.append(st_p)
            xp = _mm_post(op, w_out_a_b[l], g[1], xp, tm_p)
            xs = _mm_post(os_.astype(BF16), w_out_a_b[l], g[1], xs, tm_s)
        else:
            j = l - n_hgrn
            if kv_p is None:
                kv_p = _kv_shared(xp, kv_norm, w_c, w_kr, w_kr_sw, kv_a_norm, cs_p, sn_p, tm_p)
                kv_s = _kv_shared(xs, kv_norm, w_c, w_kr, w_kr_sw, kv_a_norm, cs_s, sn_s, tm_s)
            q_args = (g[0], w_dq_b[j], q_a_norm[j], w_q_nope[j], w_q_rope[j], w_q_rope_sw[j], w_uk_t)
            ql_p, qr_p = _mla_q(xp, *q_args, cs_p, sn_p, tm_p)
            ql_s, qr_s = _mla_q(xs, *q_args, cs_s, sn_s, tm_s)
            ol_p = _attn_prompt(ql_p, qr_p, kv_p[2], kv_p[3], batch, seq)
            ol_s = _attn_sample(ql_s.transpose(1, 0, 2), qr_s.transpose(1, 0, 2), kv_s[0], kv_s[1],
                                cache_kv_latent, cache_k_rope, page_table)
            xp = _mla_out(ol_p, w_uv, w_out_b_b[j], g[1], xp, tm_p)
            xs = _mla_out(ol_s.astype(BF16).transpose(1, 0, 2), w_uv, w_out_b_b[j], g[1], xs, tm_s)
        xp = _ffn(xp, g[2], g[3], w_ffn_in_b[l], w_ffn_out_b[l], tm_p)
        xs = _ffn(xs, g[2], g[3], w_ffn_in_b[l], w_ffn_out_b[l], tm_s)

    return (xp.reshape(batch, seq, d), xs.reshape(n_s, 1, d),
            jnp.stack(states_p), kv_p[0].reshape(batch, seq, KV_LORA), kv_p[1].reshape(batch, seq, ROPE_DIM),
            states_s, kv_s[0].reshape(n_s, 1, KV_LORA), kv_s[1].reshape(n_s, 1, ROPE_DIM))
```

```python
import functools

import jax
import jax.numpy as jnp
from jax import lax
from jax.experimental import pallas as pl
from jax.experimental.pallas import tpu as pltpu

F32 = jnp.float32
BF16 = jnp.bfloat16

D_MODEL = 1024
N_HEADS = 8
HEAD_DIM = 128
ROPE_DIM = 64
Q_LORA = 384
KV_LORA = 256
D_FF = 2816
N_HGRN = 2
EPS = 1e-6
LB_FLOOR = 1e-30
ROPE_THETA = 10000.0
ATTN_SCALE = (HEAD_DIM + ROPE_DIM) ** -0.5
ATTN_SCALE_LOG2 = ATTN_SCALE * 1.4426950408889634
MASK_VALUE = -1e30

HGRN_CHUNK = 128
FFN_COLS = 256
ATTN_TQ = 128
ATTN_TK = 256
ATTN_SAMPLE_PAGES = 16
VMEM_LIMIT = 56 * 1024 * 1024
LANES = 128

_NT = (((1,), (1,)), ((), ()))
_TN = (((0,), (0,)), ((), ()))


def _params(*sem):
    return pltpu.CompilerParams(dimension_semantics=sem, vmem_limit_bytes=VMEM_LIMIT)


def _dot(a, b):
    return jnp.dot(a, b, preferred_element_type=F32)


def _dot_nt(a, b):
    return lax.dot_general(a, b, _NT, preferred_element_type=F32)


def _rms(xf, gain):
    ms = jnp.mean(xf * xf, axis=-1, keepdims=True)
    return xf * lax.rsqrt(ms + EPS) * gain


def _silu(x):
    return x * jax.nn.sigmoid(x)


def _const_spec(shape):
    n = len(shape)
    return pl.BlockSpec(shape, lambda *_: (0,) * n)


def _norm_mm_kernel(x_ref, g_ref, w_ref, o_ref, *, cols):
    h = _rms(x_ref[...], g_ref[...]).astype(BF16)
    for c in range(0, o_ref.shape[1], cols):
        o_ref[:, c:c + cols] = _dot(h, w_ref[:, c:c + cols])


def _norm_mm(x, gain, w, tm):
    m, d = x.shape
    n = w.shape[1]
    return pl.pallas_call(
        functools.partial(_norm_mm_kernel, cols=1024),
        grid=(m // tm,),
        in_specs=[pl.BlockSpec((tm, d), lambda i: (i, 0)),
                  _const_spec((1, d)), _const_spec((d, n))],
        out_specs=pl.BlockSpec((tm, n), lambda i: (i, 0)),
        out_shape=jax.ShapeDtypeStruct((m, n), F32),
        compiler_params=_params("parallel"),
        name="norm_mm",
    )(x, gain.reshape(1, d), w)


def _mm_post_kernel(a_ref, w_ref, g_ref, x_ref, o_ref):
    y = _dot(a_ref[...], w_ref[...])
    o_ref[...] = x_ref[...] + _rms(y, g_ref[...])


def _mm_post(a, w, gain, x, tm):
    m, k = a.shape
    d = w.shape[1]
    return pl.pallas_call(
        _mm_post_kernel,
        grid=(m // tm,),
        in_specs=[pl.BlockSpec((tm, k), lambda i: (i, 0)),
                  _const_spec((k, d)), _const_spec((1, d)),
                  pl.BlockSpec((tm, d), lambda i: (i, 0))],
        out_specs=pl.BlockSpec((tm, d), lambda i: (i, 0)),
        out_shape=jax.ShapeDtypeStruct((m, d), F32),
        compiler_params=_params("parallel"),
        name="mm_post",
    )(a, w, gain.reshape(1, d), x)


def _ffn_kernel(x_ref, g_in_ref, g_out_ref, w_in_ref, w_out_ref, o_ref, a_scr):
    x = x_ref[...]
    h = _rms(x, g_in_ref[...]).astype(BF16)
    for c in range(0, D_FF, FFN_COLS):
        g = _dot(h, w_in_ref[:, c:c + FFN_COLS])
        u = _dot(h, w_in_ref[:, D_FF + c:D_FF + c + FFN_COLS])
        a_scr[:, c:c + FFN_COLS] = (_silu(g) * u).astype(BF16)
    y = _dot(a_scr[...], w_out_ref[...])
    o_ref[...] = x + _rms(y, g_out_ref[...])


def _ffn(x, g_in, g_out, w_in, w_out, tm):
    m, d = x.shape
    return pl.pallas_call(
        _ffn_kernel,
        grid=(m // tm,),
        in_specs=[pl.BlockSpec((tm, d), lambda i: (i, 0)),
                  _const_spec((1, d)), _const_spec((1, d)),
                  _const_spec((d, 2 * D_FF)), _const_spec((D_FF, d))],
        out_specs=pl.BlockSpec((tm, d), lambda i: (i, 0)),
        out_shape=jax.ShapeDtypeStruct((m, d), F32),
        scratch_shapes=[pltpu.VMEM((tm, D_FF), BF16)],
        compiler_params=_params("parallel"),
        name="ffn",
    )(x, g_in.reshape(1, d), g_out.reshape(1, d), w_in, w_out)


def _lower_bound(logits, layer):
    m = jnp.max(logits, axis=0, keepdims=True)
    e = jnp.exp(logits - m)
    p = e / jnp.sum(e, axis=0, keepdims=True)
    acc = p[0:1]
    for r in range(1, layer + 1):
        acc = acc + p[r:r + 1]
    return acc - p[0:1]


def _forget_gate(fz, lb):
    f = jnp.maximum(lb, LB_FLOOR) + (1.0 - lb) * jax.nn.sigmoid(fz)
    return jnp.minimum(f, 1.0)


def _mid_rows(b, m):
    c = b.shape[0]
    if m >= 4:
        g3 = b.reshape(c // (2 * m), 2 * m, b.shape[1])
        return jnp.broadcast_to(g3[:, m - 1:m, :], g3.shape).reshape(b.shape)
    g3 = b.reshape(c // 8, 8, b.shape[1])
    sub = lax.broadcasted_iota(jnp.int32, g3.shape, 1)

    def row(r):
        return jnp.broadcast_to(g3[:, r:r + 1, :], g3.shape)

    if m == 2:
        out = jnp.where(sub < 4, row(1), row(5))
    else:
        out = jnp.where(sub < 2, row(0), jnp.where(sub < 4, row(2), jnp.where(sub < 6, row(4), row(6))))
    return out.reshape(b.shape)


def _hgrn_prompt_kernel(z_ref, lbl_ref, gn_ref, o_ref, st_ref, state_scr, *, layer):
    c = pl.program_id(1)
    chunk = z_ref.shape[0]

    @pl.when(c == 0)
    def _():
        state_scr[...] = jnp.zeros_like(state_scr)

    lb_all = _lower_bound(lbl_ref[...], layer)
    ti = lax.broadcasted_iota(jnp.int32, (chunk, chunk), 0)
    si = lax.broadcasted_iota(jnp.int32, (chunk, chunk), 1)
    level = jnp.where(ti > si, ti ^ si, 0)
    eye = ti == si
    tril = (ti >= si).astype(F32)
    row = lax.broadcasted_iota(jnp.int32, (chunk, HEAD_DIM), 0)
    signs = []
    m = 1
    while m < chunk:
        signs.append((m, jnp.where((row & m) != 0, 1.0, -1.0)))
        m *= 2

    for h in range(N_HEADS):
        lo = h * HEAD_DIM
        hi = lo + HEAD_DIM
        qz = z_ref[:, lo:hi]
        fz = z_ref[:, D_MODEL + lo:D_MODEL + hi]
        v = z_ref[:, 2 * D_MODEL + lo:2 * D_MODEL + hi]
        gz = z_ref[:, 3 * D_MODEL + lo:3 * D_MODEL + hi]

        f = _forget_gate(fz, lb_all[:, lo:hi])
        k = 1.0 - f
        q = _silu(qz) * (HEAD_DIM ** -0.5)
        vb = v.astype(BF16)
        b = jnp.dot(tril, jnp.log2(f), precision=lax.Precision.HIGHEST, preferred_element_type=F32)

        a = jnp.where(eye, _dot_nt(q.astype(BF16), k.astype(BF16)), 0.0)
        for m, sign in signs:
            e = jnp.exp2((b - _mid_rows(b, m)) * sign)
            a = jnp.where(level >= m, _dot_nt((q * e).astype(BF16), (k * e).astype(BF16)), a)

        st = state_scr[h]
        o = _dot(a.astype(BF16), vb) + _dot_nt((q * jnp.exp2(b)).astype(BF16), st.astype(BF16))
        b_last = b[chunk - 1:chunk, :]
        kd = (k * jnp.exp2(b_last - b)).astype(BF16)
        state_scr[h] = jnp.exp2(b_last) * st + lax.dot_general(vb, kd, _TN, preferred_element_type=F32)

        o = o * lax.rsqrt(jnp.mean(o * o, axis=-1, keepdims=True) + EPS)
        o_ref[:, lo:hi] = (o * gn_ref[:, lo:hi] * _silu(gz)).astype(BF16)

    @pl.when(c == pl.num_programs(1) - 1)
    def _():
        for h in range(N_HEADS):
            st_ref[0, h] = state_scr[h].T


def _hgrn_prompt(z, lb_logits, g_norm, layer, batch, seq):
    n_chunks = seq // HGRN_CHUNK
    return pl.pallas_call(
        functools.partial(_hgrn_prompt_kernel, layer=layer),
        grid=(batch, n_chunks),
        in_specs=[pl.BlockSpec((HGRN_CHUNK, 4 * D_MODEL), lambda b, c: (b * n_chunks + c, 0)),
                  _const_spec(lb_logits.shape), _const_spec((1, D_MODEL))],
        out_specs=[pl.BlockSpec((HGRN_CHUNK, D_MODEL), lambda b, c: (b * n_chunks + c, 0)),
                   pl.BlockSpec((1, N_HEADS, HEAD_DIM, HEAD_DIM), lambda b, c: (b, 0, 0, 0))],
        out_shape=[jax.ShapeDtypeStruct((batch * seq, D_MODEL), BF16),
                   jax.ShapeDtypeStruct((batch, N_HEADS, HEAD_DIM, HEAD_DIM), F32)],
        scratch_shapes=[pltpu.VMEM((N_HEADS, HEAD_DIM, HEAD_DIM), F32)],
        compiler_params=_params("parallel", "arbitrary"),
        name="hgrn_prompt",
    )(z, lb_logits, g_norm.reshape(1, D_MODEL))


SAMPLE_GROUP = 4


def _hgrn_sample_kernel(*refs, layers, write_states):
    n = len(layers)
    z_refs = refs[:n]
    s_ref, lbl_ref, gn_ref, o_ref = refs[n:n + 4]
    so_ref = refs[n + 4] if write_states else None
    gn = gn_ref[...]
    for li, layer in enumerate(layers):
        z_ref = z_refs[li]
        lb = _lower_bound(lbl_ref[...], layer).reshape(N_HEADS, HEAD_DIM)
        z3 = z_ref[...].reshape(SAMPLE_GROUP * 4, N_HEADS, HEAD_DIM)
        part = lax.broadcasted_iota(jnp.int32, z3.shape, 0) % 4
        cols3 = jnp.where(part == 0, _silu(z3) * (HEAD_DIM ** -0.5),
                          jnp.where(part == 1, _forget_gate(z3, lb[None]), 0.0))
        cols = cols3.reshape(SAMPLE_GROUP * 4 * N_HEADS, HEAD_DIM).T
        for i in range(SAMPLE_GROUP):
            for h in range(N_HEADS):
                base = i * 4 * N_HEADS + h
                f_col = cols[:, base + N_HEADS:base + N_HEADS + 1]
                v_row = z_ref[i, 2 * N_HEADS + h:2 * N_HEADS + h + 1, :]
                s_new = f_col * s_ref[li, i, h] + (1.0 - f_col) * v_row
                if write_states:
                    so_ref[li, i, h] = s_new
                if li == n - 1:
                    q_col = cols[:, base:base + 1]
                    gz_row = z_ref[i, 3 * N_HEADS + h:3 * N_HEADS + h + 1, :]
                    o = jnp.sum(q_col * s_new, axis=0, keepdims=True)
                    o = o * lax.rsqrt(jnp.mean(o * o, axis=-1, keepdims=True) + EPS)
                    o_ref[i, h:h + 1, :] = o * gn[h:h + 1, :] * _silu(gz_row)


def _hgrn_sample(zs, state, lb_logits, g_norm, layer, write_states):
    n = zs[0].shape[0]
    layers = tuple(range(layer + 1)) if write_states else (layer,)
    assert len(zs) == len(layers)
    state_block = (len(layers), SAMPLE_GROUP, N_HEADS, HEAD_DIM, HEAD_DIM)
    first = layers[0]
    out_specs = [pl.BlockSpec((SAMPLE_GROUP, N_HEADS, HEAD_DIM), lambda i: (i, 0, 0))]
    out_shape = [jax.ShapeDtypeStruct((n, N_HEADS, HEAD_DIM), F32)]
    if write_states:
        out_specs.append(pl.BlockSpec(state_block, lambda i: (0, i, 0, 0, 0)))
        out_shape.append(jax.ShapeDtypeStruct((len(layers), n, N_HEADS, HEAD_DIM, HEAD_DIM), F32))
    outs = pl.pallas_call(
        functools.partial(_hgrn_sample_kernel, layers=layers, write_states=write_states),
        grid=(n // SAMPLE_GROUP,),
        in_specs=[pl.BlockSpec((SAMPLE_GROUP, 4 * N_HEADS, HEAD_DIM), lambda i: (i, 0, 0)) for _ in zs]
                 + [pl.BlockSpec(state_block, lambda i: (first, i, 0, 0, 0)),
                    _const_spec(lb_logits.shape), _const_spec((N_HEADS, HEAD_DIM))],
        out_specs=out_specs,
        out_shape=out_shape,
        compiler_params=_params("parallel"),
        name="hgrn_sample",
    )(*[z.reshape(n, 4 * N_HEADS, HEAD_DIM) for z in zs], state, lb_logits, g_norm.reshape(N_HEADS, HEAD_DIM))
    o = outs[0].reshape(n, D_MODEL)
    return (o, outs[1]) if write_states else (o, None)


def _kv_kernel(x_ref, g_ref, wc_ref, wkr_ref, wkrs_ref, ga_ref, cs_ref, sn_ref,
               c_ref, kr_ref, cb_ref, krb_ref):
    h = _rms(x_ref[...], g_ref[...]).astype(BF16)
    c = _rms(_dot(h, wc_ref[...]), ga_ref[...])
    kr = _dot(h, wkr_ref[...]) * cs_ref[...] + _dot(h, wkrs_ref[...]) * sn_ref[...]
    c_ref[...] = c
    kr_ref[...] = kr
    cb_ref[...] = c.astype(BF16)
    krb_ref[...] = kr.astype(BF16)


def _kv_shared(x, gain, w_c, w_kr, w_kr_swapped, gain_a, cs, sn, tm):
    m, d = x.shape
    n_pos = cs.shape[0] // tm
    return pl.pallas_call(
        _kv_kernel,
        grid=(m // tm,),
        in_specs=[pl.BlockSpec((tm, d), lambda i: (i, 0)),
                  _const_spec((1, d)), _const_spec(w_c.shape), _const_spec(w_kr.shape),
                  _const_spec(w_kr_swapped.shape), _const_spec((1, KV_LORA)),
                  pl.BlockSpec((tm, ROPE_DIM), lambda i: (i % n_pos, 0)),
                  pl.BlockSpec((tm, ROPE_DIM), lambda i: (i % n_pos, 0))],
        out_specs=[pl.BlockSpec((tm, KV_LORA), lambda i: (i, 0)),
                   pl.BlockSpec((tm, ROPE_DIM), lambda i: (i, 0)),
                   pl.BlockSpec((tm, KV_LORA), lambda i: (i, 0)),
                   pl.BlockSpec((tm, ROPE_DIM), lambda i: (i, 0))],
        out_shape=[jax.ShapeDtypeStruct((m, KV_LORA), F32),
                   jax.ShapeDtypeStruct((m, ROPE_DIM), F32),
                   jax.ShapeDtypeStruct((m, KV_LORA), BF16),
                   jax.ShapeDtypeStruct((m, ROPE_DIM), BF16)],
        compiler_params=_params("parallel"),
        name="kv_shared",
    )(x, gain.reshape(1, d), w_c, w_kr, w_kr_swapped, gain_a.reshape(1, KV_LORA), cs, sn)


def _mla_q_kernel(x_ref, g_ref, wdq_ref, gq_ref, wn_ref, wr_ref, wrs_ref, wukt_ref, cs_ref, sn_ref,
                  ql_ref, qr_ref):
    h = _rms(x_ref[...], g_ref[...]).astype(BF16)
    qa = _rms(_dot(h, wdq_ref[...]), gq_ref[...]).astype(BF16)
    q_nope = _dot(qa, wn_ref[...]).astype(BF16)
    q_rope = _dot(qa, wr_ref[...]) * cs_ref[...] + _dot(qa, wrs_ref[...]) * sn_ref[...]
    for hd in range(N_HEADS):
        ql_ref[hd] = _dot(q_nope[:, hd * HEAD_DIM:(hd + 1) * HEAD_DIM], wukt_ref[hd]).astype(BF16)
        qr_ref[hd] = q_rope[:, hd * ROPE_DIM:(hd + 1) * ROPE_DIM].astype(BF16)


def _mla_q(x, gain, w_dq, gain_q, w_nope, w_rope, w_rope_swapped, w_uk_t, cs, sn, tm):
    m, d = x.shape
    n_pos = cs.shape[0] // tm
    return pl.pallas_call(
        _mla_q_kernel,
        grid=(m // tm,),
        in_specs=[pl.BlockSpec((tm, d), lambda i: (i, 0)),
                  _const_spec((1, d)), _const_spec(w_dq.shape), _const_spec((1, Q_LORA)),
                  _const_spec(w_nope.shape), _const_spec(w_rope.shape),
                  _const_spec(w_rope_swapped.shape), _const_spec(w_uk_t.shape),
                  pl.BlockSpec((tm, N_HEADS * ROPE_DIM), lambda i: (i % n_pos, 0)),
                  pl.BlockSpec((tm, N_HEADS * ROPE_DIM), lambda i: (i % n_pos, 0))],
        out_specs=[pl.BlockSpec((N_HEADS, tm, KV_LORA), lambda i: (0, i, 0)),
                   pl.BlockSpec((N_HEADS, tm, ROPE_DIM), lambda i: (0, i, 0))],
        out_shape=[jax.ShapeDtypeStruct((N_HEADS, m, KV_LORA), BF16),
                   jax.ShapeDtypeStruct((N_HEADS, m, ROPE_DIM), BF16)],
        compiler_params=_params("parallel"),
        name="mla_q",
    )(x, gain.reshape(1, d), w_dq, gain_q.reshape(1, Q_LORA), w_nope, w_rope, w_rope_swapped, w_uk_t, cs, sn)


def _attn_prompt_kernel(ql_ref, qr_ref, c_ref, kr_ref, o_ref, s_scr, m_scr, l_scr, acc_scr):
    i = pl.program_id(1)
    rows = N_HEADS * ATTN_TQ
    m_scr[...] = jnp.full_like(m_scr, MASK_VALUE)
    l_scr[...] = jnp.zeros_like(l_scr)
    acc_scr[...] = jnp.zeros_like(acc_scr)
    ql = ql_ref[...].reshape(rows, KV_LORA)
    qr = qr_ref[...].reshape(rows, ROPE_DIM)
    n_blocks = i // (ATTN_TK // ATTN_TQ) + 1

    def key_start(j):
        return j * ATTN_TK if isinstance(j, int) else pl.multiple_of(j * ATTN_TK, ATTN_TK)

    def scores(j, slot):
        start = key_start(j)
        s_scr[slot] = (_dot_nt(ql, c_ref[pl.ds(start, ATTN_TK), :])
                       + _dot_nt(qr, kr_ref[pl.ds(start, ATTN_TK), :]))

    def softmax_pv(j, slot, masked):
        start = key_start(j)
        s = s_scr[slot]
        if masked:
            shape = (N_HEADS, ATTN_TQ, ATTN_TK)
            q_pos = i * ATTN_TQ + lax.broadcasted_iota(jnp.int32, shape, 1).reshape(rows, ATTN_TK)
            k_pos = start + lax.broadcasted_iota(jnp.int32, (rows, ATTN_TK), 1)
            s = jnp.where(k_pos <= q_pos, s, MASK_VALUE)
        tiles = [s[:, t:t + LANES] for t in range(0, ATTN_TK, LANES)]
        tile_max = functools.reduce(jnp.maximum, tiles)
        m_prev = m_scr[...]
        m_new = jnp.maximum(m_prev, jnp.max(tile_max, axis=-1, keepdims=True))
        alpha = jnp.exp2((m_prev - m_new) * ATTN_SCALE_LOG2)
        ps = [jnp.exp2((t - m_new) * ATTN_SCALE_LOG2) for t in tiles]
        l_scr[...] = alpha * l_scr[...] + functools.reduce(jnp.add, ps)
        pv = _dot(jnp.concatenate(ps, axis=1).astype(BF16), c_ref[pl.ds(start, ATTN_TK), :])
        for t in range(0, KV_LORA, LANES):
            acc_scr[:, t:t + LANES] = alpha * acc_scr[:, t:t + LANES] + pv[:, t:t + LANES]
        m_scr[...] = m_new

    def pair(jj, carry):
        j = 2 * jj
        scores(j + 1, 1)
        softmax_pv(j, 0, False)
        scores(j + 2, 0)
        softmax_pv(j + 1, 1, False)
        return carry

    last = n_blocks - 1
    scores(0, 0)
    lax.fori_loop(0, last // 2, pair, 0)

    @pl.when(last % 2 == 1)
    def _():
        scores(last, 1)
        softmax_pv(last - 1, 0, False)
        softmax_pv(last, 1, True)

    @pl.when(last % 2 == 0)
    def _():
        softmax_pv(last, 0, True)

    inv_l = 1.0 / jnp.sum(l_scr[...], axis=-1, keepdims=True)
    o_ref[...] = (acc_scr[...] * inv_l).reshape(o_ref.shape).astype(BF16)


def _attn_prompt(ql, qr, cb, krb, batch, seq):
    nq = seq // ATTN_TQ
    return pl.pallas_call(
        _attn_prompt_kernel,
        grid=(batch, nq),
        in_specs=[pl.BlockSpec((N_HEADS, ATTN_TQ, KV_LORA), lambda b, i: (0, b * nq + i, 0)),
                  pl.BlockSpec((N_HEADS, ATTN_TQ, ROPE_DIM), lambda b, i: (0, b * nq + i, 0)),
                  pl.BlockSpec((seq, KV_LORA), lambda b, i: (b, 0)),
                  pl.BlockSpec((seq, ROPE_DIM), lambda b, i: (b, 0))],
        out_specs=pl.BlockSpec((N_HEADS, ATTN_TQ, KV_LORA), lambda b, i: (0, b * nq + i, 0)),
        out_shape=jax.ShapeDtypeStruct(ql.shape, BF16),
        scratch_shapes=[pltpu.VMEM((2, N_HEADS * ATTN_TQ, ATTN_TK), F32),
                        pltpu.VMEM((N_HEADS * ATTN_TQ, LANES), F32),
                        pltpu.VMEM((N_HEADS * ATTN_TQ, LANES), F32),
                        pltpu.VMEM((N_HEADS * ATTN_TQ, KV_LORA), F32)],
        compiler_params=_params("parallel", "arbitrary"),
        name="attn_prompt",
    )(ql, qr, cb, krb)


def _attn_sample_kernel(pt_ref, ql_ref, qr_ref, cn_ref, krn_ref, cache_c, cache_krt, o_ref,
                        c_buf, krt_buf, key_scr, s_scr, sem):
    b = pl.program_id(0)
    n_seq = pl.num_programs(0)
    n_pages, page = c_buf.shape[1], c_buf.shape[2]
    slot = b % 2

    def page_copies(seq, to_slot, p):
        pg = pt_ref[seq, p]
        return (pltpu.make_async_copy(cache_c.at[pg], c_buf.at[to_slot, p], sem.at[to_slot, 0]),
                pltpu.make_async_copy(cache_krt.at[pg], krt_buf.at[to_slot, p], sem.at[to_slot, 1]))

    def for_all_pages(seq, to_slot, action):
        def body(p, carry):
            for cp in page_copies(seq, to_slot, p):
                action(cp)
            return carry
        lax.fori_loop(0, n_pages, body, 0)

    @pl.when(b == 0)
    def _():
        for_all_pages(0, 0, lambda cp: cp.start())

    @pl.when(b + 1 < n_seq)
    def _():
        for_all_pages(b + 1, 1 - slot, lambda cp: cp.start())

    for_all_pages(b, slot, lambda cp: cp.wait())

    ql = ql_ref[0]
    qr = qr_ref[0]
    keys = ATTN_SAMPLE_PAGES * page
    for j in range(n_pages // ATTN_SAMPLE_PAGES):
        first = j * ATTN_SAMPLE_PAGES
        kc = c_buf[slot, first:first + ATTN_SAMPLE_PAGES].reshape(keys, KV_LORA).astype(BF16)
        key_scr[j * keys:(j + 1) * keys, :] = kc
        krt = jnp.concatenate([krt_buf[slot, first + p].astype(BF16) for p in range(ATTN_SAMPLE_PAGES)],
                              axis=1)
        s_scr[:, j * keys:(j + 1) * keys] = _dot_nt(ql, kc) + _dot(qr, krt)

    cn = cn_ref[0].astype(BF16).astype(F32)
    krn = krn_ref[0].astype(BF16).astype(F32)
    s_new = (jnp.sum(ql.astype(F32) * cn, axis=-1, keepdims=True)
             + jnp.sum(qr.astype(F32) * krn, axis=-1, keepdims=True))
    s = s_scr[...]
    m = jnp.maximum(s_new, jnp.max(s, axis=-1, keepdims=True))
    p = jnp.exp2((s - m) * ATTN_SCALE_LOG2)
    p_new = jnp.exp2((s_new - m) * ATTN_SCALE_LOG2)
    l = p_new + jnp.sum(p, axis=-1, keepdims=True)
    acc = p_new * cn + _dot(p.astype(BF16), key_scr[...])
    o_ref[0] = acc / l


def _attn_sample(ql, qr, c_new, kr_new, cache_c, cache_krt, page_table):
    n, n_pages = page_table.shape
    page = cache_c.shape[1]
    assert n_pages % ATTN_SAMPLE_PAGES == 0
    grid_spec = pltpu.PrefetchScalarGridSpec(
        num_scalar_prefetch=1,
        grid=(n,),
        in_specs=[pl.BlockSpec((1, N_HEADS, KV_LORA), lambda b, pt: (b, 0, 0)),
                  pl.BlockSpec((1, N_HEADS, ROPE_DIM), lambda b, pt: (b, 0, 0)),
                  pl.BlockSpec((1, 1, KV_LORA), lambda b, pt: (b, 0, 0)),
                  pl.BlockSpec((1, 1, ROPE_DIM), lambda b, pt: (b, 0, 0)),
                  pl.BlockSpec(memory_space=pl.ANY), pl.BlockSpec(memory_space=pl.ANY)],
        out_specs=pl.BlockSpec((1, N_HEADS, KV_LORA), lambda b, pt: (b, 0, 0)),
        scratch_shapes=[pltpu.VMEM((2, n_pages, page, KV_LORA), F32),
                        pltpu.VMEM((2, n_pages, ROPE_DIM, page), F32),
                        pltpu.VMEM((n_pages * page, KV_LORA), BF16),
                        pltpu.VMEM((N_HEADS, n_pages * page), F32),
                        pltpu.SemaphoreType.DMA((2, 2))],
    )
    return pl.pallas_call(
        _attn_sample_kernel,
        grid_spec=grid_spec,
        out_shape=jax.ShapeDtypeStruct((n, N_HEADS, KV_LORA), F32),
        compiler_params=_params("arbitrary"),
        name="attn_sample",
    )(page_table, ql, qr, c_new.reshape(n, 1, KV_LORA), kr_new.reshape(n, 1, ROPE_DIM), cache_c, cache_krt)


def _mla_out_kernel(ol_ref, wuv_ref, wout_ref, g_ref, x_ref, o_ref, o_scr):
    for hd in range(N_HEADS):
        o_scr[:, hd * HEAD_DIM:(hd + 1) * HEAD_DIM] = _dot(ol_ref[hd], wuv_ref[hd]).astype(BF16)
    y = _dot(o_scr[...], wout_ref[...])
    o_ref[...] = x_ref[...] + _rms(y, g_ref[...])


def _mla_out(o_lat, w_uv, w_out, gain, x, tm):
    m, d = x.shape
    return pl.pallas_call(
        _mla_out_kernel,
        grid=(m // tm,),
        in_specs=[pl.BlockSpec((N_HEADS, tm, KV_LORA), lambda i: (0, i, 0)),
                  _const_spec(w_uv.shape), _const_spec(w_out.shape), _const_spec((1, d)),
                  pl.BlockSpec((tm, d), lambda i: (i, 0))],
        out_specs=pl.BlockSpec((tm, d), lambda i: (i, 0)),
        out_shape=jax.ShapeDtypeStruct((m, d), F32),
        scratch_shapes=[pltpu.VMEM((tm, d), BF16)],
        compiler_params=_params("parallel"),
        name="mla_out",
    )(o_lat, w_uv, w_out, gain.reshape(1, d), x)


def _rope_tables(pos):
    half = ROPE_DIM // 2
    inv = ROPE_THETA ** (-jnp.arange(half, dtype=F32) / half)
    ang = pos.astype(F32)[:, None] * inv[None, :]
    cos, sin = jnp.cos(ang), jnp.sin(ang)
    return jnp.concatenate([cos, cos], axis=-1), jnp.concatenate([-sin, sin], axis=-1)


def _swap_halves(w):
    half = w.shape[-1] // 2
    return jnp.concatenate([w[..., half:], w[..., :half]], axis=-1)


def kernel(x_prompt, x_sample, state_hgrn, cache_kv_latent, cache_k_rope, page_table, norm_gains, w_ffn_in, w_ffn_out, w_in_a, lb_logits, g_norm_a, w_out_a, kv_norm, w_dkv, kv_a_norm, w_ukv, w_dq, q_a_norm, w_uq, w_out_b):
    batch, seq, d = x_prompt.shape
    n_s = x_sample.shape[0]
    depth = norm_gains.shape[0]
    n_hgrn = w_in_a.shape[0]
    past_len = page_table.shape[1] * cache_kv_latent.shape[1]
    tm_p, tm_s = 512, n_s

    w_ffn_in_b = w_ffn_in.astype(BF16)
    w_ffn_out_b = w_ffn_out.astype(BF16)
    w_in_a_b = w_in_a.astype(BF16)
    w_out_a_b = w_out_a.astype(BF16)
    w_out_b_b = w_out_b.astype(BF16)
    w_dq_b = w_dq.astype(BF16)
    w_c = w_dkv[:, :KV_LORA].astype(BF16)
    w_kr = w_dkv[:, KV_LORA:].astype(BF16)
    w_kr_sw = _swap_halves(w_kr)
    w_ukv_h = w_ukv.reshape(KV_LORA, N_HEADS, 2 * HEAD_DIM).astype(BF16)
    w_uk_t = w_ukv_h[:, :, :HEAD_DIM].transpose(1, 2, 0)
    w_uv = w_ukv_h[:, :, HEAD_DIM:].transpose(1, 0, 2)
    w_uq_h = w_uq.reshape(-1, Q_LORA, N_HEADS, HEAD_DIM + ROPE_DIM).astype(BF16)
    n_mla = w_uq_h.shape[0]
    w_q_nope = w_uq_h[..., :HEAD_DIM].reshape(n_mla, Q_LORA, N_HEADS * HEAD_DIM)
    w_q_rope = w_uq_h[..., HEAD_DIM:].reshape(n_mla, Q_LORA, N_HEADS * ROPE_DIM)
    w_q_rope_sw = _swap_halves(w_uq_h[..., HEAD_DIM:]).reshape(n_mla, Q_LORA, N_HEADS * ROPE_DIM)
    cache_krt = jnp.swapaxes(cache_k_rope, 1, 2)

    cs_p, sn_p = _rope_tables(jnp.arange(seq, dtype=jnp.int32))
    cs_s, sn_s = _rope_tables(jnp.full((n_s,), past_len, dtype=jnp.int32))
    cs_p8, sn_p8, cs_s8, sn_s8 = (jnp.tile(t, (1, N_HEADS)) for t in (cs_p, sn_p, cs_s, sn_s))

    xp = x_prompt.reshape(batch * seq, d)
    xs = x_sample.reshape(n_s, d)
    states_p, zs_all, states_s = [], [], None
    kv_p = kv_s = None
    for l in range(depth):
        g = norm_gains[l]
        if l < n_hgrn:
            zp = _norm_mm(xp, g[0], w_in_a_b[l], tm_p)
            zs = _norm_mm(xs, g[0], w_in_a_b[l], tm_s)
            op, st_p = _hgrn_prompt(zp, lb_logits, g_norm_a[l], l, batch, seq)
            zs_all.append(zs)
            last = l == n_hgrn - 1
            os_, states_s = _hgrn_sample(zs_all if last else [zs], state_hgrn, lb_logits, g_norm_a[l], l, last)
            states_p.append(st_p)
            xp = _mm_post(op, w_out_a_b[l], g[1], xp, tm_p)
            xs = _mm_post(os_.astype(BF16), w_out_a_b[l], g[1], xs, tm_s)
        else:
            j = l - n_hgrn
            if kv_p is None:
                kv_p = _kv_shared(xp, kv_norm, w_c, w_kr, w_kr_sw, kv_a_norm, cs_p, sn_p, tm_p)
                kv_s = _kv_shared(xs, kv_norm, w_c, w_kr, w_kr_sw, kv_a_norm, cs_s, sn_s, tm_s)
            q_args = (g[0], w_dq_b[j], q_a_norm[j], w_q_nope[j], w_q_rope[j], w_q_rope_sw[j], w_uk_t)
            ql_p, qr_p = _mla_q(xp, *q_args, cs_p8, sn_p8, tm_p)
            ql_s, qr_s = _mla_q(xs, *q_args, cs_s8, sn_s8, tm_s)
            ol_p = _attn_prompt(ql_p, qr_p, kv_p[2], kv_p[3], batch, seq)
            ol_s = _attn_sample(ql_s.transpose(1, 0, 2), qr_s.transpose(1, 0, 2), kv_s[0], kv_s[1],
                                cache_kv_latent, cache_krt, page_table)
            xp = _mla_out(ol_p, w_uv, w_out_b_b[j], g[1], xp, tm_p)
            xs = _mla_out(ol_s.astype(BF16).transpose(1, 0, 2), w_uv, w_out_b_b[j], g[1], xs, tm_s)
        xp = _ffn(xp, g[2], g[3], w_ffn_in_b[l], w_ffn_out_b[l], tm_p)
        xs = _ffn(xs, g[2], g[3], w_ffn_in_b[l], w_ffn_out_b[l], tm_s)

    return (xp.reshape(batch, seq, d), xs.reshape(n_s, 1, d),
            jnp.stack(states_p), kv_p[0].reshape(batch, seq, KV_LORA), kv_p[1].reshape(batch, seq, ROPE_DIM),
            states_s, kv_s[0].reshape(n_s, 1, KV_LORA), kv_s[1].reshape(n_s, 1, ROPE_DIM))
```

```python
import functools

import jax
import jax.numpy as jnp
from jax import lax
from jax.experimental import pallas as pl
from jax.experimental.pallas import tpu as pltpu

F32 = jnp.float32
BF16 = jnp.bfloat16

D_MODEL = 1024
N_HEADS = 8
HEAD_DIM = 128
ROPE_DIM = 64
Q_LORA = 384
KV_LORA = 256
D_FF = 2816
N_HGRN = 2
EPS = 1e-6
LB_FLOOR = 1e-30
ROPE_THETA = 10000.0
ATTN_SCALE = (HEAD_DIM + ROPE_DIM) ** -0.5
ATTN_SCALE_LOG2 = ATTN_SCALE * 1.4426950408889634
MASK_VALUE = -1e30

HGRN_CHUNK = 128
FFN_COLS = 256
ATTN_TQ = 256
ATTN_TK = 256
ATTN_SAMPLE_PAGES = 16
VMEM_LIMIT = 56 * 1024 * 1024
LANES = 128

_NT = (((1,), (1,)), ((), ()))
_TN = (((0,), (0,)), ((), ()))


def _params(*sem):
    return pltpu.CompilerParams(dimension_semantics=sem, vmem_limit_bytes=VMEM_LIMIT)


def _dot(a, b):
    return jnp.dot(a, b, preferred_element_type=F32)


def _dot_nt(a, b):
    return lax.dot_general(a, b, _NT, preferred_element_type=F32)


def _rms(xf, gain):
    ms = jnp.mean(xf * xf, axis=-1, keepdims=True)
    return xf * lax.rsqrt(ms + EPS) * gain


def _silu(x):
    return x * jax.nn.sigmoid(x)


def _const_spec(shape):
    n = len(shape)
    return pl.BlockSpec(shape, lambda *_: (0,) * n)


def _norm_mm_kernel(x_ref, g_ref, w_ref, o_ref, *, cols):
    h = _rms(x_ref[...], g_ref[...]).astype(BF16)
    for c in range(0, o_ref.shape[1], cols):
        o_ref[:, c:c + cols] = _dot(h, w_ref[:, c:c + cols])


def _norm_mm(x, gain, w, tm):
    m, d = x.shape
    n = w.shape[1]
    return pl.pallas_call(
        functools.partial(_norm_mm_kernel, cols=1024),
        grid=(m // tm,),
        in_specs=[pl.BlockSpec((tm, d), lambda i: (i, 0)),
                  _const_spec((1, d)), _const_spec((d, n))],
        out_specs=pl.BlockSpec((tm, n), lambda i: (i, 0)),
        out_shape=jax.ShapeDtypeStruct((m, n), F32),
        compiler_params=_params("parallel"),
        name="norm_mm",
    )(x, gain.reshape(1, d), w)


def _mm_post_kernel(a_ref, w_ref, g_ref, x_ref, o_ref):
    y = _dot(a_ref[...], w_ref[...])
    o_ref[...] = x_ref[...] + _rms(y, g_ref[...])


def _mm_post(a, w, gain, x, tm):
    m, k = a.shape
    d = w.shape[1]
    return pl.pallas_call(
        _mm_post_kernel,
        grid=(m // tm,),
        in_specs=[pl.BlockSpec((tm, k), lambda i: (i, 0)),
                  _const_spec((k, d)), _const_spec((1, d)),
                  pl.BlockSpec((tm, d), lambda i: (i, 0))],
        out_specs=pl.BlockSpec((tm, d), lambda i: (i, 0)),
        out_shape=jax.ShapeDtypeStruct((m, d), F32),
        compiler_params=_params("parallel"),
        name="mm_post",
    )(a, w, gain.reshape(1, d), x)


def _ffn_kernel(x_ref, g_in_ref, g_out_ref, w_in_ref, w_out_ref, o_ref, a_scr):
    x = x_ref[...]
    h = _rms(x, g_in_ref[...]).astype(BF16)
    for c in range(0, D_FF, FFN_COLS):
        g = _dot(h, w_in_ref[:, c:c + FFN_COLS])
        u = _dot(h, w_in_ref[:, D_FF + c:D_FF + c + FFN_COLS])
        a_scr[:, c:c + FFN_COLS] = (_silu(g) * u).astype(BF16)
    y = _dot(a_scr[...], w_out_ref[...])
    o_ref[...] = x + _rms(y, g_out_ref[...])


def _ffn(x, g_in, g_out, w_in, w_out, tm):
    m, d = x.shape
    return pl.pallas_call(
        _ffn_kernel,
        grid=(m // tm,),
        in_specs=[pl.BlockSpec((tm, d), lambda i: (i, 0)),
                  _const_spec((1, d)), _const_spec((1, d)),
                  _const_spec((d, 2 * D_FF)), _const_spec((D_FF, d))],
        out_specs=pl.BlockSpec((tm, d), lambda i: (i, 0)),
        out_shape=jax.ShapeDtypeStruct((m, d), F32),
        scratch_shapes=[pltpu.VMEM((tm, D_FF), BF16)],
        compiler_params=_params("parallel"),
        name="ffn",
    )(x, g_in.reshape(1, d), g_out.reshape(1, d), w_in, w_out)


def _lower_bound(logits, layer):
    m = jnp.max(logits, axis=0, keepdims=True)
    e = jnp.exp(logits - m)
    p = e / jnp.sum(e, axis=0, keepdims=True)
    acc = p[0:1]
    for r in range(1, layer + 1):
        acc = acc + p[r:r + 1]
    return acc - p[0:1]


def _forget_gate(fz, lb):
    f = jnp.maximum(lb, LB_FLOOR) + (1.0 - lb) * jax.nn.sigmoid(fz)
    return jnp.minimum(f, 1.0)


def _mid_rows(b, m):
    c = b.shape[0]
    if m >= 4:
        g3 = b.reshape(c // (2 * m), 2 * m, b.shape[1])
        return jnp.broadcast_to(g3[:, m - 1:m, :], g3.shape).reshape(b.shape)
    g3 = b.reshape(c // 8, 8, b.shape[1])
    sub = lax.broadcasted_iota(jnp.int32, g3.shape, 1)

    def row(r):
        return jnp.broadcast_to(g3[:, r:r + 1, :], g3.shape)

    if m == 2:
        out = jnp.where(sub < 4, row(1), row(5))
    else:
        out = jnp.where(sub < 2, row(0), jnp.where(sub < 4, row(2), jnp.where(sub < 6, row(4), row(6))))
    return out.reshape(b.shape)


def _hgrn_prompt_kernel(z_ref, lbl_ref, gn_ref, o_ref, st_ref, state_scr, *, layer):
    c = pl.program_id(1)
    chunk = z_ref.shape[0]

    @pl.when(c == 0)
    def _():
        state_scr[...] = jnp.zeros_like(state_scr)

    lb_all = _lower_bound(lbl_ref[...], layer)
    ti = lax.broadcasted_iota(jnp.int32, (chunk, chunk), 0)
    si = lax.broadcasted_iota(jnp.int32, (chunk, chunk), 1)
    level = jnp.where(ti > si, ti ^ si, 0)
    eye = ti == si
    tril = (ti >= si).astype(BF16)
    tril3 = jnp.concatenate([tril, tril, tril], axis=1)
    row = lax.broadcasted_iota(jnp.int32, (chunk, HEAD_DIM), 0)
    signs = []
    m = 1
    while m < chunk:
        signs.append((m, jnp.where((row & m) != 0, 1.0, -1.0)))
        m *= 2

    def intra_chunk(h):
        lo = h * HEAD_DIM
        hi = lo + HEAD_DIM
        qz = z_ref[:, lo:hi]
        fz = z_ref[:, D_MODEL + lo:D_MODEL + hi]
        f = _forget_gate(fz, lb_all[:, lo:hi])
        k = 1.0 - f
        q = _silu(qz) * (HEAD_DIM ** -0.5)
        g = jnp.log2(f)
        g_hi = g.astype(BF16)
        rest = g - g_hi.astype(F32)
        g_mid = rest.astype(BF16)
        g_lo = (rest - g_mid.astype(F32)).astype(BF16)
        b = _dot(tril3, jnp.concatenate([g_hi, g_mid, g_lo], axis=0))
        qb = q.astype(BF16)
        kb = k.astype(BF16)
        a = jnp.where(eye, _dot_nt(qb, kb), 0.0)
        for m, sign in signs:
            e = jnp.exp2((b - _mid_rows(b, m)) * sign).astype(BF16)
            a = jnp.where(level >= m, _dot_nt(qb * e, kb * e), a)
        return q, k, b, a.astype(BF16)

    def read_out_and_update(h, q, k, b, a):
        lo = h * HEAD_DIM
        hi = lo + HEAD_DIM
        vb = z_ref[:, 2 * D_MODEL + lo:2 * D_MODEL + hi].astype(BF16)
        st = state_scr[h]
        o = _dot(a, vb) + _dot_nt((q * jnp.exp2(b)).astype(BF16), st.astype(BF16))
        b_last = b[chunk - 1:chunk, :]
        kd = (k * jnp.exp2(b_last - b)).astype(BF16)
        state_scr[h] = jnp.exp2(b_last) * st + lax.dot_general(vb, kd, _TN, preferred_element_type=F32)
        o = o * lax.rsqrt(jnp.mean(o * o, axis=-1, keepdims=True) + EPS)
        gz = z_ref[:, 3 * D_MODEL + lo:3 * D_MODEL + hi]
        o_ref[:, lo:hi] = (o * gn_ref[:, lo:hi] * _silu(gz)).astype(BF16)

    pending = intra_chunk(0)
    for h in range(N_HEADS):
        current = pending
        if h + 1 < N_HEADS:
            pending = intra_chunk(h + 1)
        read_out_and_update(h, *current)

    @pl.when(c == pl.num_programs(1) - 1)
    def _():
        for h in range(N_HEADS):
            st_ref[0, h] = state_scr[h].T


def _hgrn_prompt(z, lb_logits, g_norm, layer, batch, seq):
    n_chunks = seq // HGRN_CHUNK
    return pl.pallas_call(
        functools.partial(_hgrn_prompt_kernel, layer=layer),
        grid=(batch, n_chunks),
        in_specs=[pl.BlockSpec((HGRN_CHUNK, 4 * D_MODEL), lambda b, c: (b * n_chunks + c, 0)),
                  _const_spec(lb_logits.shape), _const_spec((1, D_MODEL))],
        out_specs=[pl.BlockSpec((HGRN_CHUNK, D_MODEL), lambda b, c: (b * n_chunks + c, 0)),
                   pl.BlockSpec((1, N_HEADS, HEAD_DIM, HEAD_DIM), lambda b, c: (b, 0, 0, 0))],
        out_shape=[jax.ShapeDtypeStruct((batch * seq, D_MODEL), BF16),
                   jax.ShapeDtypeStruct((batch, N_HEADS, HEAD_DIM, HEAD_DIM), F32)],
        scratch_shapes=[pltpu.VMEM((N_HEADS, HEAD_DIM, HEAD_DIM), F32)],
        compiler_params=_params("parallel", "arbitrary"),
        name="hgrn_prompt",
    )(z, lb_logits, g_norm.reshape(1, D_MODEL))


SAMPLE_GROUP = 4


def _hgrn_sample_kernel(*refs, layers, write_states):
    n = len(layers)
    z_refs = refs[:n]
    s_ref, lbl_ref, gn_ref, o_ref = refs[n:n + 4]
    so_ref = refs[n + 4] if write_states else None
    sb_scr, o_scr = refs[-2:]
    gn = gn_ref[...]
    for li, layer in enumerate(layers):
        z_ref = z_refs[li]
        lb = _lower_bound(lbl_ref[...], layer).reshape(N_HEADS, HEAD_DIM)
        z3 = z_ref[...].reshape(SAMPLE_GROUP * 4, N_HEADS, HEAD_DIM)
        f3 = _forget_gate(z3, lb[None])
        cols = f3.reshape(SAMPLE_GROUP * 4 * N_HEADS, HEAD_DIM).T
        mixes = li == n - 1
        for i in range(SAMPLE_GROUP):
            for h in range(N_HEADS):
                base = i * 4 * N_HEADS + N_HEADS + h
                f_col = cols[:, base:base + 1]
                v_row = z_ref[i, 2 * N_HEADS + h:2 * N_HEADS + h + 1, :]
                s_new = f_col * (s_ref[li, i, h] - v_row) + v_row
                if write_states:
                    so_ref[li, i, h] = s_new
                if mixes:
                    sb_scr[i * N_HEADS + h] = s_new.astype(BF16)
        if mixes:
            for i in range(SAMPLE_GROUP):
                q_rows = (_silu(z_ref[i, 0:N_HEADS, :]) * (HEAD_DIM ** -0.5)).astype(BF16)
                for h in range(N_HEADS):
                    r = i * N_HEADS + h
                    o_scr[r:r + 1, :] = _dot(q_rows, sb_scr[r])[h:h + 1, :]
            o = o_scr[...].reshape(SAMPLE_GROUP, N_HEADS, HEAD_DIM)
            o = o * lax.rsqrt(jnp.mean(o * o, axis=-1, keepdims=True) + EPS)
            o_ref[...] = o * gn[None] * _silu(z_ref[:, 3 * N_HEADS:4 * N_HEADS, :])


def _hgrn_sample(zs, state, lb_logits, g_norm, layer, write_states):
    n = zs[0].shape[0]
    layers = tuple(range(layer + 1)) if write_states else (layer,)
    assert len(zs) == len(layers)
    state_block = (len(layers), SAMPLE_GROUP, N_HEADS, HEAD_DIM, HEAD_DIM)
    first = layers[0]
    out_specs = [pl.BlockSpec((SAMPLE_GROUP, N_HEADS, HEAD_DIM), lambda i: (i, 0, 0))]
    out_shape = [jax.ShapeDtypeStruct((n, N_HEADS, HEAD_DIM), F32)]
    if write_states:
        out_specs.append(pl.BlockSpec(state_block, lambda i: (0, i, 0, 0, 0)))
        out_shape.append(jax.ShapeDtypeStruct((len(layers), n, N_HEADS, HEAD_DIM, HEAD_DIM), F32))
    outs = pl.pallas_call(
        functools.partial(_hgrn_sample_kernel, layers=layers, write_states=write_states),
        grid=(n // SAMPLE_GROUP,),
        in_specs=[pl.BlockSpec((SAMPLE_GROUP, 4 * N_HEADS, HEAD_DIM), lambda i: (i, 0, 0)) for _ in zs]
                 + [pl.BlockSpec(state_block, lambda i: (first, i, 0, 0, 0)),
                    _const_spec(lb_logits.shape), _const_spec((N_HEADS, HEAD_DIM))],
        out_specs=out_specs,
        out_shape=out_shape,
        scratch_shapes=[pltpu.VMEM((SAMPLE_GROUP * N_HEADS, HEAD_DIM, HEAD_DIM), BF16),
                        pltpu.VMEM((SAMPLE_GROUP * N_HEADS, HEAD_DIM), F32)],
        compiler_params=_params("parallel"),
        name="hgrn_sample",
    )(*[z.reshape(n, 4 * N_HEADS, HEAD_DIM) for z in zs], state, lb_logits, g_norm.reshape(N_HEADS, HEAD_DIM))
    o = outs[0].reshape(n, D_MODEL)
    return (o, outs[1]) if write_states else (o, None)


def _kv_kernel(x_ref, g_ref, wc_ref, wkr_ref, wkrs_ref, ga_ref, cs_ref, sn_ref,
               c_ref, kr_ref, cb_ref, krb_ref):
    h = _rms(x_ref[...], g_ref[...]).astype(BF16)
    c = _rms(_dot(h, wc_ref[...]), ga_ref[...])
    kr = _dot(h, wkr_ref[...]) * cs_ref[...] + _dot(h, wkrs_ref[...]) * sn_ref[...]
    c_ref[...] = c
    kr_ref[...] = kr
    cb_ref[...] = c.astype(BF16)
    krb_ref[...] = kr.astype(BF16)


def _kv_shared(x, gain, w_c, w_kr, w_kr_swapped, gain_a, cs, sn, tm):
    m, d = x.shape
    n_pos = cs.shape[0] // tm
    return pl.pallas_call(
        _kv_kernel,
        grid=(m // tm,),
        in_specs=[pl.BlockSpec((tm, d), lambda i: (i, 0)),
                  _const_spec((1, d)), _const_spec(w_c.shape), _const_spec(w_kr.shape),
                  _const_spec(w_kr_swapped.shape), _const_spec((1, KV_LORA)),
                  pl.BlockSpec((tm, ROPE_DIM), lambda i: (i % n_pos, 0)),
                  pl.BlockSpec((tm, ROPE_DIM), lambda i: (i % n_pos, 0))],
        out_specs=[pl.BlockSpec((tm, KV_LORA), lambda i: (i, 0)),
                   pl.BlockSpec((tm, ROPE_DIM), lambda i: (i, 0)),
                   pl.BlockSpec((tm, KV_LORA), lambda i: (i, 0)),
                   pl.BlockSpec((tm, ROPE_DIM), lambda i: (i, 0))],
        out_shape=[jax.ShapeDtypeStruct((m, KV_LORA), F32),
                   jax.ShapeDtypeStruct((m, ROPE_DIM), F32),
                   jax.ShapeDtypeStruct((m, KV_LORA), BF16),
                   jax.ShapeDtypeStruct((m, ROPE_DIM), BF16)],
        compiler_params=_params("parallel"),
        name="kv_shared",
    )(x, gain.reshape(1, d), w_c, w_kr, w_kr_swapped, gain_a.reshape(1, KV_LORA), cs, sn)


def _mla_q_kernel(x_ref, g_ref, wdq_ref, gq_ref, wn_ref, wr_ref, wrs_ref, wukt_ref, cs_ref, sn_ref,
                  ql_ref, qr_ref):
    h = _rms(x_ref[...], g_ref[...]).astype(BF16)
    qa = _rms(_dot(h, wdq_ref[...]), gq_ref[...]).astype(BF16)
    q_nope = _dot(qa, wn_ref[...]).astype(BF16)
    q_rope = _dot(qa, wr_ref[...]) * cs_ref[...] + _dot(qa, wrs_ref[...]) * sn_ref[...]
    for hd in range(N_HEADS):
        ql_ref[hd] = _dot(q_nope[:, hd * HEAD_DIM:(hd + 1) * HEAD_DIM], wukt_ref[hd]).astype(BF16)
        qr_ref[hd] = q_rope[:, hd * ROPE_DIM:(hd + 1) * ROPE_DIM].astype(BF16)


def _mla_q(x, gain, w_dq, gain_q, w_nope, w_rope, w_rope_swapped, w_uk_t, cs, sn, tm):
    m, d = x.shape
    n_pos = cs.shape[0] // tm
    return pl.pallas_call(
        _mla_q_kernel,
        grid=(m // tm,),
        in_specs=[pl.BlockSpec((tm, d), lambda i: (i, 0)),
                  _const_spec((1, d)), _const_spec(w_dq.shape), _const_spec((1, Q_LORA)),
                  _const_spec(w_nope.shape), _const_spec(w_rope.shape),
                  _const_spec(w_rope_swapped.shape), _const_spec(w_uk_t.shape),
                  pl.BlockSpec((tm, N_HEADS * ROPE_DIM), lambda i: (i % n_pos, 0)),
                  pl.BlockSpec((tm, N_HEADS * ROPE_DIM), lambda i: (i % n_pos, 0))],
        out_specs=[pl.BlockSpec((N_HEADS, tm, KV_LORA), lambda i: (0, i, 0)),
                   pl.BlockSpec((N_HEADS, tm, ROPE_DIM), lambda i: (0, i, 0))],
        out_shape=[jax.ShapeDtypeStruct((N_HEADS, m, KV_LORA), BF16),
                   jax.ShapeDtypeStruct((N_HEADS, m, ROPE_DIM), BF16)],
        compiler_params=_params("parallel"),
        name="mla_q",
    )(x, gain.reshape(1, d), w_dq, gain_q.reshape(1, Q_LORA), w_nope, w_rope, w_rope_swapped, w_uk_t, cs, sn)


def _attn_prompt_kernel(ql_ref, qr_ref, c_ref, kr_ref, o_ref, s_scr, m_scr, l_scr, acc_scr):
    i = pl.program_id(1)
    rows = N_HEADS * ATTN_TQ
    m_scr[...] = jnp.full_like(m_scr, MASK_VALUE)
    l_scr[...] = jnp.zeros_like(l_scr)
    acc_scr[...] = jnp.zeros_like(acc_scr)
    ql = ql_ref[...].reshape(rows, KV_LORA)
    qr = qr_ref[...].reshape(rows, ROPE_DIM)
    n_blocks = i // (ATTN_TK // ATTN_TQ) + 1

    def key_start(j):
        return j * ATTN_TK if isinstance(j, int) else pl.multiple_of(j * ATTN_TK, ATTN_TK)

    def scores(j, slot):
        start = key_start(j)
        s_scr[slot] = (_dot_nt(ql, c_ref[pl.ds(start, ATTN_TK), :])
                       + _dot_nt(qr, kr_ref[pl.ds(start, ATTN_TK), :]))

    def softmax_pv(j, slot, masked):
        start = key_start(j)
        s = s_scr[slot]
        if masked:
            shape = (N_HEADS, ATTN_TQ, ATTN_TK)
            q_pos = i * ATTN_TQ + lax.broadcasted_iota(jnp.int32, shape, 1).reshape(rows, ATTN_TK)
            k_pos = start + lax.broadcasted_iota(jnp.int32, (rows, ATTN_TK), 1)
            s = jnp.where(k_pos <= q_pos, s, MASK_VALUE)
        tiles = [s[:, t:t + LANES] for t in range(0, ATTN_TK, LANES)]
        tile_max = functools.reduce(jnp.maximum, tiles)
        m_prev = m_scr[...]
        m_new = jnp.maximum(m_prev, jnp.max(tile_max, axis=-1, keepdims=True))
        alpha = jnp.exp2((m_prev - m_new) * ATTN_SCALE_LOG2)
        ps = [jnp.exp2((t - m_new) * ATTN_SCALE_LOG2) for t in tiles]
        l_scr[...] = alpha * l_scr[...] + functools.reduce(jnp.add, ps)
        pv = _dot(jnp.concatenate(ps, axis=1).astype(BF16), c_ref[pl.ds(start, ATTN_TK), :])
        for t in range(0, KV_LORA, LANES):
            acc_scr[:, t:t + LANES] = alpha * acc_scr[:, t:t + LANES] + pv[:, t:t + LANES]
        m_scr[...] = m_new

    def pair(jj, carry):
        j = 2 * jj
        scores(j + 1, 1)
        softmax_pv(j, 0, False)
        scores(j + 2, 0)
        softmax_pv(j + 1, 1, False)
        return carry

    last = n_blocks - 1
    scores(0, 0)
    lax.fori_loop(0, last // 2, pair, 0)

    @pl.when(last % 2 == 1)
    def _():
        scores(last, 1)
        softmax_pv(last - 1, 0, False)
        softmax_pv(last, 1, True)

    @pl.when(last % 2 == 0)
    def _():
        softmax_pv(last, 0, True)

    inv_l = 1.0 / jnp.sum(l_scr[...], axis=-1, keepdims=True)
    o_ref[...] = (acc_scr[...] * inv_l).reshape(o_ref.shape).astype(BF16)


def _attn_prompt(ql, qr, cb, krb, batch, seq):
    nq = seq // ATTN_TQ
    return pl.pallas_call(
        _attn_prompt_kernel,
        grid=(batch, nq),
        in_specs=[pl.BlockSpec((N_HEADS, ATTN_TQ, KV_LORA), lambda b, i: (0, b * nq + i, 0)),
                  pl.BlockSpec((N_HEADS, ATTN_TQ, ROPE_DIM), lambda b, i: (0, b * nq + i, 0)),
                  pl.BlockSpec((seq, KV_LORA), lambda b, i: (b, 0)),
                  pl.BlockSpec((seq, ROPE_DIM), lambda b, i: (b, 0))],
        out_specs=pl.BlockSpec((N_HEADS, ATTN_TQ, KV_LORA), lambda b, i: (0, b * nq + i, 0)),
        out_shape=jax.ShapeDtypeStruct(ql.shape, BF16),
        scratch_shapes=[pltpu.VMEM((2, N_HEADS * ATTN_TQ, ATTN_TK), F32),
                        pltpu.VMEM((N_HEADS * ATTN_TQ, LANES), F32),
                        pltpu.VMEM((N_HEADS * ATTN_TQ, LANES), F32),
                        pltpu.VMEM((N_HEADS * ATTN_TQ, KV_LORA), F32)],
        compiler_params=_params("parallel", "arbitrary"),
        name="attn_prompt",
    )(ql, qr, cb, krb)


def _attn_sample_kernel(pt_ref, ql_ref, qr_ref, cn_ref, krn_ref, cache_c, cache_krt, o_ref,
                        c_buf, krt_buf, key_scr, s_scr, sem):
    b = pl.program_id(0)
    n_seq = pl.num_programs(0)
    n_pages, page = c_buf.shape[1], c_buf.shape[2]
    slot = b % 2

    def page_copies(seq, to_slot, p):
        pg = pt_ref[seq, p]
        return (pltpu.make_async_copy(cache_c.at[pg], c_buf.at[to_slot, p], sem.at[to_slot, 0]),
                pltpu.make_async_copy(cache_krt.at[pg], krt_buf.at[to_slot, p], sem.at[to_slot, 1]))

    def for_all_pages(seq, to_slot, action):
        def body(p, carry):
            for cp in page_copies(seq, to_slot, p):
                action(cp)
            return carry
        lax.fori_loop(0, n_pages, body, 0)

    @pl.when(b == 0)
    def _():
        for_all_pages(0, 0, lambda cp: cp.start())

    @pl.when(b + 1 < n_seq)
    def _():
        for_all_pages(b + 1, 1 - slot, lambda cp: cp.start())

    for_all_pages(b, slot, lambda cp: cp.wait())

    ql = ql_ref[0]
    qr = qr_ref[0]
    keys = ATTN_SAMPLE_PAGES * page
    for j in range(n_pages // ATTN_SAMPLE_PAGES):
        first = j * ATTN_SAMPLE_PAGES
        kc = c_buf[slot, first:first + ATTN_SAMPLE_PAGES].reshape(keys, KV_LORA).astype(BF16)
        key_scr[j * keys:(j + 1) * keys, :] = kc
        krt = jnp.concatenate([krt_buf[slot, first + p].astype(BF16) for p in range(ATTN_SAMPLE_PAGES)],
                              axis=1)
        s_scr[:, j * keys:(j + 1) * keys] = _dot_nt(ql, kc) + _dot(qr, krt)

    cn = cn_ref[0].astype(BF16).astype(F32)
    krn = krn_ref[0].astype(BF16).astype(F32)
    s_new = (jnp.sum(ql.astype(F32) * cn, axis=-1, keepdims=True)
             + jnp.sum(qr.astype(F32) * krn, axis=-1, keepdims=True))
    s = s_scr[...]
    m = jnp.maximum(s_new, jnp.max(s, axis=-1, keepdims=True))
    p = jnp.exp2((s - m) * ATTN_SCALE_LOG2)
    p_new = jnp.exp2((s_new - m) * ATTN_SCALE_LOG2)
    l = p_new + jnp.sum(p, axis=-1, keepdims=True)
    acc = p_new * cn + _dot(p.astype(BF16), key_scr[...])
    o_ref[0] = acc / l


def _attn_sample(ql, qr, c_new, kr_new, cache_c, cache_krt, page_table):
    n, n_pages = page_table.shape
    page = cache_c.shape[1]
    assert n_pages % ATTN_SAMPLE_PAGES == 0
    grid_spec = pltpu.PrefetchScalarGridSpec(
        num_scalar_prefetch=1,
        grid=(n,),
        in_specs=[pl.BlockSpec((1, N_HEADS, KV_LORA), lambda b, pt: (b, 0, 0)),
                  pl.BlockSpec((1, N_HEADS, ROPE_DIM), lambda b, pt: (b, 0, 0)),
                  pl.BlockSpec((1, 1, KV_LORA), lambda b, pt: (b, 0, 0)),
                  pl.BlockSpec((1, 1, ROPE_DIM), lambda b, pt: (b, 0, 0)),
                  pl.BlockSpec(memory_space=pl.ANY), pl.BlockSpec(memory_space=pl.ANY)],
        out_specs=pl.BlockSpec((1, N_HEADS, KV_LORA), lambda b, pt: (b, 0, 0)),
        scratch_shapes=[pltpu.VMEM((2, n_pages, page, KV_LORA), F32),
                        pltpu.VMEM((2, n_pages, ROPE_DIM, page), F32),
                        pltpu.VMEM((n_pages * page, KV_LORA), BF16),
                        pltpu.VMEM((N_HEADS, n_pages * page), F32),
                        pltpu.SemaphoreType.DMA((2, 2))],
    )
    return pl.pallas_call(
        _attn_sample_kernel,
        grid_spec=grid_spec,
        out_shape=jax.ShapeDtypeStruct((n, N_HEADS, KV_LORA), F32),
        compiler_params=_params("arbitrary"),
        name="attn_sample",
    )(page_table, ql, qr, c_new.reshape(n, 1, KV_LORA), kr_new.reshape(n, 1, ROPE_DIM), cache_c, cache_krt)


def _mla_out_kernel(ol_ref, wuv_ref, wout_ref, g_ref, x_ref, o_ref, o_scr):
    for hd in range(N_HEADS):
        o_scr[:, hd * HEAD_DIM:(hd + 1) * HEAD_DIM] = _dot(ol_ref[hd], wuv_ref[hd]).astype(BF16)
    y = _dot(o_scr[...], wout_ref[...])
    o_ref[...] = x_ref[...] + _rms(y, g_ref[...])


def _mla_out(o_lat, w_uv, w_out, gain, x, tm):
    m, d = x.shape
    return pl.pallas_call(
        _mla_out_kernel,
        grid=(m // tm,),
        in_specs=[pl.BlockSpec((N_HEADS, tm, KV_LORA), lambda i: (0, i, 0)),
                  _const_spec(w_uv.shape), _const_spec(w_out.shape), _const_spec((1, d)),
                  pl.BlockSpec((tm, d), lambda i: (i, 0))],
        out_specs=pl.BlockSpec((tm, d), lambda i: (i, 0)),
        out_shape=jax.ShapeDtypeStruct((m, d), F32),
        scratch_shapes=[pltpu.VMEM((tm, d), BF16)],
        compiler_params=_params("parallel"),
        name="mla_out",
    )(o_lat, w_uv, w_out, gain.reshape(1, d), x)


def _rope_tables(pos):
    half = ROPE_DIM // 2
    inv = ROPE_THETA ** (-jnp.arange(half, dtype=F32) / half)
    ang = pos.astype(F32)[:, None] * inv[None, :]
    cos, sin = jnp.cos(ang), jnp.sin(ang)
    return jnp.concatenate([cos, cos], axis=-1), jnp.concatenate([-sin, sin], axis=-1)


def _swap_halves(w):
    half = w.shape[-1] // 2
    return jnp.concatenate([w[..., half:], w[..., :half]], axis=-1)


def kernel(x_prompt, x_sample, state_hgrn, cache_kv_latent, cache_k_rope, page_table, norm_gains, w_ffn_in, w_ffn_out, w_in_a, lb_logits, g_norm_a, w_out_a, kv_norm, w_dkv, kv_a_norm, w_ukv, w_dq, q_a_norm, w_uq, w_out_b):
    batch, seq, d = x_prompt.shape
    n_s = x_sample.shape[0]
    depth = norm_gains.shape[0]
    n_hgrn = w_in_a.shape[0]
    past_len = page_table.shape[1] * cache_kv_latent.shape[1]
    tm_p, tm_s = 512, n_s

    w_ffn_in_b = w_ffn_in.astype(BF16)
    w_ffn_out_b = w_ffn_out.astype(BF16)
    w_in_a_b = w_in_a.astype(BF16)
    w_out_a_b = w_out_a.astype(BF16)
    w_out_b_b = w_out_b.astype(BF16)
    w_dq_b = w_dq.astype(BF16)
    w_c = w_dkv[:, :KV_LORA].astype(BF16)
    w_kr = w_dkv[:, KV_LORA:].astype(BF16)
    w_kr_sw = _swap_halves(w_kr)
    w_ukv_h = w_ukv.reshape(KV_LORA, N_HEADS, 2 * HEAD_DIM).astype(BF16)
    w_uk_t = w_ukv_h[:, :, :HEAD_DIM].transpose(1, 2, 0)
    w_uv = w_ukv_h[:, :, HEAD_DIM:].transpose(1, 0, 2)
    w_uq_h = w_uq.reshape(-1, Q_LORA, N_HEADS, HEAD_DIM + ROPE_DIM).astype(BF16)
    n_mla = w_uq_h.shape[0]
    w_q_nope = w_uq_h[..., :HEAD_DIM].reshape(n_mla, Q_LORA, N_HEADS * HEAD_DIM)
    w_q_rope = w_uq_h[..., HEAD_DIM:].reshape(n_mla, Q_LORA, N_HEADS * ROPE_DIM)
    w_q_rope_sw = _swap_halves(w_uq_h[..., HEAD_DIM:]).reshape(n_mla, Q_LORA, N_HEADS * ROPE_DIM)
    cache_krt = jnp.swapaxes(cache_k_rope, 1, 2)

    cs_p, sn_p = _rope_tables(jnp.arange(seq, dtype=jnp.int32))
    cs_s, sn_s = _rope_tables(jnp.full((n_s,), past_len, dtype=jnp.int32))
    cs_p8, sn_p8, cs_s8, sn_s8 = (jnp.tile(t, (1, N_HEADS)) for t in (cs_p, sn_p, cs_s, sn_s))

    xp = x_prompt.reshape(batch * seq, d)
    xs = x_sample.reshape(n_s, d)
    states_p, zs_all, states_s = [], [], None
    kv_p = kv_s = None
    for l in range(depth):
        g = norm_gains[l]
        if l < n_hgrn:
            zp = _norm_mm(xp, g[0], w_in_a_b[l], tm_p)
            zs = _norm_mm(xs, g[0], w_in_a_b[l], tm_s)
            op, st_p = _hgrn_prompt(zp, lb_logits, g_norm_a[l], l, batch, seq)
            zs_all.append(zs)
            last = l == n_hgrn - 1
            os_, states_s = _hgrn_sample(zs_all if last else [zs], state_hgrn, lb_logits, g_norm_a[l], l, last)
            states_p.append(st_p)
            xp = _mm_post(op, w_out_a_b[l], g[1], xp, tm_p)
            xs = _mm_post(os_.astype(BF16), w_out_a_b[l], g[1], xs, tm_s)
        else:
            j = l - n_hgrn
            if kv_p is None:
                kv_p = _kv_shared(xp, kv_norm, w_c, w_kr, w_kr_sw, kv_a_norm, cs_p, sn_p, tm_p)
                kv_s = _kv_shared(xs, kv_norm, w_c, w_kr, w_kr_sw, kv_a_norm, cs_s, sn_s, tm_s)
            q_args = (g[0], w_dq_b[j], q_a_norm[j], w_q_nope[j], w_q_rope[j], w_q_rope_sw[j], w_uk_t)
            ql_p, qr_p = _mla_q(xp, *q_args, cs_p8, sn_p8, tm_p)
            ql_s, qr_s = _mla_q(xs, *q_args, cs_s8, sn_s8, tm_s)
            ol_p = _attn_prompt(ql_p, qr_p, kv_p[2], kv_p[3], batch, seq)
            ol_s = _attn_sample(ql_s.transpose(1, 0, 2), qr_s.transpose(1, 0, 2), kv_s[0], kv_s[1],
                                cache_kv_latent, cache_krt, page_table)
            xp = _mla_out(ol_p, w_uv, w_out_b_b[j], g[1], xp, tm_p)
            xs = _mla_out(ol_s.astype(BF16).transpose(1, 0, 2), w_uv, w_out_b_b[j], g[1], xs, tm_s)
        xp = _ffn(xp, g[2], g[3], w_ffn_in_b[l], w_ffn_out_b[l], tm_p)
        xs = _ffn(xs, g[2], g[3], w_ffn_in_b[l], w_ffn_out_b[l], tm_s)

    return (xp.reshape(batch, seq, d), xs.reshape(n_s, 1, d),
            jnp.stack(states_p), kv_p[0].reshape(batch, seq, KV_LORA), kv_p[1].reshape(batch, seq, ROPE_DIM),
            states_s, kv_s[0].reshape(n_s, 1, KV_LORA), kv_s[1].reshape(n_s, 1, ROPE_DIM))
```

```python
import functools

import jax
import jax.numpy as jnp
from jax import lax
from jax.experimental import pallas as pl
from jax.experimental.pallas import tpu as pltpu

F32 = jnp.float32
BF16 = jnp.bfloat16

D_MODEL = 1024
N_HEADS = 8
HEAD_DIM = 128
ROPE_DIM = 64
Q_LORA = 384
KV_LORA = 256
D_FF = 2816
N_HGRN = 2
EPS = 1e-6
LB_FLOOR = 1e-30
ROPE_THETA = 10000.0
ATTN_SCALE = (HEAD_DIM + ROPE_DIM) ** -0.5
ATTN_SCALE_LOG2 = ATTN_SCALE * 1.4426950408889634
MASK_VALUE = -1e30

HGRN_CHUNK = 128
FFN_COLS = 256
ATTN_TQ = 512
ATTN_TK = 512
ATTN_HEADS = 2
ATTN_SAMPLE_PAGES = 16
VMEM_LIMIT = 56 * 1024 * 1024
LANES = 128

_NT = (((1,), (1,)), ((), ()))
_TN = (((0,), (0,)), ((), ()))


def _params(*sem):
    return pltpu.CompilerParams(dimension_semantics=sem, vmem_limit_bytes=VMEM_LIMIT)


def _dot(a, b):
    return jnp.dot(a, b, preferred_element_type=F32)


def _dot_nt(a, b):
    return lax.dot_general(a, b, _NT, preferred_element_type=F32)


def _rms(xf, gain):
    ms = jnp.mean(xf * xf, axis=-1, keepdims=True)
    return xf * lax.rsqrt(ms + EPS) * gain


def _silu(x):
    return x * jax.nn.sigmoid(x)


def _const_spec(shape):
    n = len(shape)
    return pl.BlockSpec(shape, lambda *_: (0,) * n)


def _norm_mm_kernel(x_ref, g_ref, w_ref, o_ref, *, cols):
    h = _rms(x_ref[...], g_ref[...]).astype(BF16)
    for c in range(0, o_ref.shape[1], cols):
        o_ref[:, c:c + cols] = _dot(h, w_ref[:, c:c + cols])


def _norm_mm(x, gain, w, tm):
    m, d = x.shape
    n = w.shape[1]
    return pl.pallas_call(
        functools.partial(_norm_mm_kernel, cols=1024),
        grid=(m // tm,),
        in_specs=[pl.BlockSpec((tm, d), lambda i: (i, 0)),
                  _const_spec((1, d)), _const_spec((d, n))],
        out_specs=pl.BlockSpec((tm, n), lambda i: (i, 0)),
        out_shape=jax.ShapeDtypeStruct((m, n), F32),
        compiler_params=_params("parallel"),
        name="norm_mm",
    )(x, gain.reshape(1, d), w)


def _mm_post_kernel(a_ref, w_ref, g_ref, x_ref, o_ref):
    y = _dot(a_ref[...], w_ref[...])
    o_ref[...] = x_ref[...] + _rms(y, g_ref[...])


def _mm_post(a, w, gain, x, tm):
    m, k = a.shape
    d = w.shape[1]
    return pl.pallas_call(
        _mm_post_kernel,
        grid=(m // tm,),
        in_specs=[pl.BlockSpec((tm, k), lambda i: (i, 0)),
                  _const_spec((k, d)), _const_spec((1, d)),
                  pl.BlockSpec((tm, d), lambda i: (i, 0))],
        out_specs=pl.BlockSpec((tm, d), lambda i: (i, 0)),
        out_shape=jax.ShapeDtypeStruct((m, d), F32),
        compiler_params=_params("parallel"),
        name="mm_post",
    )(a, w, gain.reshape(1, d), x)


def _ffn_kernel(x_ref, g_in_ref, g_out_ref, w_in_ref, w_out_ref, o_ref, a_scr):
    x = x_ref[...]
    h = _rms(x, g_in_ref[...]).astype(BF16)
    for c in range(0, D_FF, FFN_COLS):
        g = _dot(h, w_in_ref[:, c:c + FFN_COLS])
        u = _dot(h, w_in_ref[:, D_FF + c:D_FF + c + FFN_COLS])
        a_scr[:, c:c + FFN_COLS] = (_silu(g) * u).astype(BF16)
    y = _dot(a_scr[...], w_out_ref[...])
    o_ref[...] = x + _rms(y, g_out_ref[...])


def _ffn(x, g_in, g_out, w_in, w_out, tm):
    m, d = x.shape
    return pl.pallas_call(
        _ffn_kernel,
        grid=(m // tm,),
        in_specs=[pl.BlockSpec((tm, d), lambda i: (i, 0)),
                  _const_spec((1, d)), _const_spec((1, d)),
                  _const_spec((d, 2 * D_FF)), _const_spec((D_FF, d))],
        out_specs=pl.BlockSpec((tm, d), lambda i: (i, 0)),
        out_shape=jax.ShapeDtypeStruct((m, d), F32),
        scratch_shapes=[pltpu.VMEM((tm, D_FF), BF16)],
        compiler_params=_params("parallel"),
        name="ffn",
    )(x, g_in.reshape(1, d), g_out.reshape(1, d), w_in, w_out)


def _lower_bound(logits, layer):
    m = jnp.max(logits, axis=0, keepdims=True)
    e = jnp.exp(logits - m)
    p = e / jnp.sum(e, axis=0, keepdims=True)
    acc = p[0:1]
    for r in range(1, layer + 1):
        acc = acc + p[r:r + 1]
    return acc - p[0:1]


def _forget_gate(fz, lb):
    f = jnp.maximum(lb, LB_FLOOR) + (1.0 - lb) * jax.nn.sigmoid(fz)
    return jnp.minimum(f, 1.0)


def _mid_rows(b, m):
    c = b.shape[0]
    if m >= 4:
        g3 = b.reshape(c // (2 * m), 2 * m, b.shape[1])
        return jnp.broadcast_to(g3[:, m - 1:m, :], g3.shape).reshape(b.shape)
    g3 = b.reshape(c // 8, 8, b.shape[1])
    sub = lax.broadcasted_iota(jnp.int32, g3.shape, 1)

    def row(r):
        return jnp.broadcast_to(g3[:, r:r + 1, :], g3.shape)

    if m == 2:
        out = jnp.where(sub < 4, row(1), row(5))
    else:
        out = jnp.where(sub < 2, row(0), jnp.where(sub < 4, row(2), jnp.where(sub < 6, row(4), row(6))))
    return out.reshape(b.shape)


def _hgrn_prompt_kernel(z_ref, lbl_ref, gn_ref, o_ref, st_ref, state_scr, *, layer):
    c = pl.program_id(1)
    chunk = z_ref.shape[0]

    @pl.when(c == 0)
    def _():
        state_scr[...] = jnp.zeros_like(state_scr)

    lb_all = _lower_bound(lbl_ref[...], layer)
    ti = lax.broadcasted_iota(jnp.int32, (chunk, chunk), 0)
    si = lax.broadcasted_iota(jnp.int32, (chunk, chunk), 1)
    level = jnp.where(ti > si, ti ^ si, 0)
    eye = ti == si
    tril = (ti >= si).astype(BF16)
    tril3 = jnp.concatenate([tril, tril, tril], axis=1)
    row = lax.broadcasted_iota(jnp.int32, (chunk, HEAD_DIM), 0)
    signs = []
    m = 1
    while m < chunk:
        signs.append((m, jnp.where((row & m) != 0, 1.0, -1.0)))
        m *= 2

    def intra_chunk(h):
        lo = h * HEAD_DIM
        hi = lo + HEAD_DIM
        qz = z_ref[:, lo:hi]
        fz = z_ref[:, D_MODEL + lo:D_MODEL + hi]
        f = _forget_gate(fz, lb_all[:, lo:hi])
        k = 1.0 - f
        q = _silu(qz) * (HEAD_DIM ** -0.5)
        g = jnp.log2(f)
        g_hi = g.astype(BF16)
        rest = g - g_hi.astype(F32)
        g_mid = rest.astype(BF16)
        g_lo = (rest - g_mid.astype(F32)).astype(BF16)
        b = _dot(tril3, jnp.concatenate([g_hi, g_mid, g_lo], axis=0))
        qb = q.astype(BF16)
        kb = k.astype(BF16)
        a = jnp.where(eye, _dot_nt(qb, kb), 0.0)
        for m, sign in signs:
            e = jnp.exp2((b - _mid_rows(b, m)) * sign).astype(BF16)
            a = jnp.where(level >= m, _dot_nt(qb * e, kb * e), a)
        return q, k, b, a.astype(BF16)

    def read_out_and_update(h, q, k, b, a):
        lo = h * HEAD_DIM
        hi = lo + HEAD_DIM
        vb = z_ref[:, 2 * D_MODEL + lo:2 * D_MODEL + hi].astype(BF16)
        st = state_scr[h]
        o = _dot(a, vb) + _dot_nt((q * jnp.exp2(b)).astype(BF16), st.astype(BF16))
        b_last = b[chunk - 1:chunk, :]
        kd = (k * jnp.exp2(b_last - b)).astype(BF16)
        state_scr[h] = jnp.exp2(b_last) * st + lax.dot_general(vb, kd, _TN, preferred_element_type=F32)
        o = o * lax.rsqrt(jnp.mean(o * o, axis=-1, keepdims=True) + EPS)
        gz = z_ref[:, 3 * D_MODEL + lo:3 * D_MODEL + hi]
        o_ref[:, lo:hi] = (o * gn_ref[:, lo:hi] * _silu(gz)).astype(BF16)

    pending = intra_chunk(0)
    for h in range(N_HEADS):
        current = pending
        if h + 1 < N_HEADS:
            pending = intra_chunk(h + 1)
        read_out_and_update(h, *current)

    @pl.when(c == pl.num_programs(1) - 1)
    def _():
        for h in range(N_HEADS):
            st_ref[0, h] = state_scr[h].T


def _hgrn_prompt(z, lb_logits, g_norm, layer, batch, seq):
    n_chunks = seq // HGRN_CHUNK
    return pl.pallas_call(
        functools.partial(_hgrn_prompt_kernel, layer=layer),
        grid=(batch, n_chunks),
        in_specs=[pl.BlockSpec((HGRN_CHUNK, 4 * D_MODEL), lambda b, c: (b * n_chunks + c, 0)),
                  _const_spec(lb_logits.shape), _const_spec((1, D_MODEL))],
        out_specs=[pl.BlockSpec((HGRN_CHUNK, D_MODEL), lambda b, c: (b * n_chunks + c, 0)),
                   pl.BlockSpec((1, N_HEADS, HEAD_DIM, HEAD_DIM), lambda b, c: (b, 0, 0, 0))],
        out_shape=[jax.ShapeDtypeStruct((batch * seq, D_MODEL), BF16),
                   jax.ShapeDtypeStruct((batch, N_HEADS, HEAD_DIM, HEAD_DIM), F32)],
        scratch_shapes=[pltpu.VMEM((N_HEADS, HEAD_DIM, HEAD_DIM), F32)],
        compiler_params=_params("parallel", "arbitrary"),
        name="hgrn_prompt",
    )(z, lb_logits, g_norm.reshape(1, D_MODEL))


SAMPLE_GROUP = 4


def _hgrn_sample_kernel(*refs, layers, write_states):
    n = len(layers)
    z_refs = refs[:n]
    s_ref, lbl_ref, gn_ref, o_ref = refs[n:n + 4]
    so_ref = refs[n + 4] if write_states else None
    sb_scr, o_scr = refs[-2:]
    gn = gn_ref[...]
    for li, layer in enumerate(layers):
        z_ref = z_refs[li]
        lb = _lower_bound(lbl_ref[...], layer).reshape(N_HEADS, HEAD_DIM)
        z3 = z_ref[...].reshape(SAMPLE_GROUP * 4, N_HEADS, HEAD_DIM)
        f3 = _forget_gate(z3, lb[None])
        cols = f3.reshape(SAMPLE_GROUP * 4 * N_HEADS, HEAD_DIM).T
        mixes = li == n - 1
        for i in range(SAMPLE_GROUP):
            for h in range(N_HEADS):
                base = i * 4 * N_HEADS + N_HEADS + h
                f_col = cols[:, base:base + 1]
                v_row = z_ref[i, 2 * N_HEADS + h:2 * N_HEADS + h + 1, :]
                s_new = f_col * (s_ref[li, i, h] - v_row) + v_row
                if write_states:
                    so_ref[li, i, h] = s_new
                if mixes:
                    sb_scr[i * N_HEADS + h] = s_new.astype(BF16)
        if mixes:
            for i in range(SAMPLE_GROUP):
                q_rows = (_silu(z_ref[i, 0:N_HEADS, :]) * (HEAD_DIM ** -0.5)).astype(BF16)
                for h in range(N_HEADS):
                    r = i * N_HEADS + h
                    o_scr[r:r + 1, :] = _dot(q_rows, sb_scr[r])[h:h + 1, :]
            o = o_scr[...].reshape(SAMPLE_GROUP, N_HEADS, HEAD_DIM)
            o = o * lax.rsqrt(jnp.mean(o * o, axis=-1, keepdims=True) + EPS)
            o_ref[...] = o * gn[None] * _silu(z_ref[:, 3 * N_HEADS:4 * N_HEADS, :])


def _hgrn_sample(zs, state, lb_logits, g_norm, layer, write_states):
    n = zs[0].shape[0]
    layers = tuple(range(layer + 1)) if write_states else (layer,)
    assert len(zs) == len(layers)
    state_block = (len(layers), SAMPLE_GROUP, N_HEADS, HEAD_DIM, HEAD_DIM)
    first = layers[0]
    out_specs = [pl.BlockSpec((SAMPLE_GROUP, N_HEADS, HEAD_DIM), lambda i: (i, 0, 0))]
    out_shape = [jax.ShapeDtypeStruct((n, N_HEADS, HEAD_DIM), F32)]
    if write_states:
        out_specs.append(pl.BlockSpec(state_block, lambda i: (0, i, 0, 0, 0)))
        out_shape.append(jax.ShapeDtypeStruct((len(layers), n, N_HEADS, HEAD_DIM, HEAD_DIM), F32))
    outs = pl.pallas_call(
        functools.partial(_hgrn_sample_kernel, layers=layers, write_states=write_states),
        grid=(n // SAMPLE_GROUP,),
        in_specs=[pl.BlockSpec((SAMPLE_GROUP, 4 * N_HEADS, HEAD_DIM), lambda i: (i, 0, 0)) for _ in zs]
                 + [pl.BlockSpec(state_block, lambda i: (first, i, 0, 0, 0)),
                    _const_spec(lb_logits.shape), _const_spec((N_HEADS, HEAD_DIM))],
        out_specs=out_specs,
        out_shape=out_shape,
        scratch_shapes=[pltpu.VMEM((SAMPLE_GROUP * N_HEADS, HEAD_DIM, HEAD_DIM), BF16),
                        pltpu.VMEM((SAMPLE_GROUP * N_HEADS, HEAD_DIM), F32)],
        compiler_params=_params("parallel"),
        name="hgrn_sample",
    )(*[z.reshape(n, 4 * N_HEADS, HEAD_DIM) for z in zs], state, lb_logits, g_norm.reshape(N_HEADS, HEAD_DIM))
    o = outs[0].reshape(n, D_MODEL)
    return (o, outs[1]) if write_states else (o, None)


def _kv_kernel(x_ref, g_ref, wc_ref, wkr_ref, wkrs_ref, ga_ref, cs_ref, sn_ref,
               c_ref, kr_ref, cb_ref, krb_ref):
    h = _rms(x_ref[...], g_ref[...]).astype(BF16)
    c = _rms(_dot(h, wc_ref[...]), ga_ref[...])
    kr = _dot(h, wkr_ref[...]) * cs_ref[...] + _dot(h, wkrs_ref[...]) * sn_ref[...]
    c_ref[...] = c
    kr_ref[...] = kr
    cb_ref[...] = c.astype(BF16)
    krb_ref[...] = kr.astype(BF16)


def _kv_shared(x, gain, w_c, w_kr, w_kr_swapped, gain_a, cs, sn, tm):
    m, d = x.shape
    n_pos = cs.shape[0] // tm
    return pl.pallas_call(
        _kv_kernel,
        grid=(m // tm,),
        in_specs=[pl.BlockSpec((tm, d), lambda i: (i, 0)),
                  _const_spec((1, d)), _const_spec(w_c.shape), _const_spec(w_kr.shape),
                  _const_spec(w_kr_swapped.shape), _const_spec((1, KV_LORA)),
                  pl.BlockSpec((tm, ROPE_DIM), lambda i: (i % n_pos, 0)),
                  pl.BlockSpec((tm, ROPE_DIM), lambda i: (i % n_pos, 0))],
        out_specs=[pl.BlockSpec((tm, KV_LORA), lambda i: (i, 0)),
                   pl.BlockSpec((tm, ROPE_DIM), lambda i: (i, 0)),
                   pl.BlockSpec((tm, KV_LORA), lambda i: (i, 0)),
                   pl.BlockSpec((tm, ROPE_DIM), lambda i: (i, 0))],
        out_shape=[jax.ShapeDtypeStruct((m, KV_LORA), F32),
                   jax.ShapeDtypeStruct((m, ROPE_DIM), F32),
                   jax.ShapeDtypeStruct((m, KV_LORA), BF16),
                   jax.ShapeDtypeStruct((m, ROPE_DIM), BF16)],
        compiler_params=_params("parallel"),
        name="kv_shared",
    )(x, gain.reshape(1, d), w_c, w_kr, w_kr_swapped, gain_a.reshape(1, KV_LORA), cs, sn)


def _mla_q_kernel(x_ref, g_ref, wdq_ref, gq_ref, wn_ref, wr_ref, wrs_ref, cs_ref, sn_ref, *rest, absorbed):
    h = _rms(x_ref[...], g_ref[...]).astype(BF16)
    qa = _rms(_dot(h, wdq_ref[...]), gq_ref[...]).astype(BF16)
    q_nope = _dot(qa, wn_ref[...]).astype(BF16)
    q_rope = (_dot(qa, wr_ref[...]) * cs_ref[...] + _dot(qa, wrs_ref[...]) * sn_ref[...]).astype(BF16)
    for hd in range(N_HEADS):
        nope = q_nope[:, hd * HEAD_DIM:(hd + 1) * HEAD_DIM]
        rope = q_rope[:, hd * ROPE_DIM:(hd + 1) * ROPE_DIM]
        if absorbed:
            wukt_ref, ql_ref, qr_ref = rest
            ql_ref[hd] = _dot(nope, wukt_ref[hd]).astype(BF16)
            qr_ref[hd] = rope
        else:
            (q_ref,) = rest
            q_ref[hd, :, 0:HEAD_DIM] = nope
            q_ref[hd, :, HEAD_DIM:HEAD_DIM + ROPE_DIM] = rope


def _mla_q(x, gain, w_dq, gain_q, w_nope, w_rope, w_rope_swapped, cs, sn, tm, w_uk_t=None):
    m, d = x.shape
    n_pos = cs.shape[0] // tm
    absorbed = w_uk_t is not None
    in_specs = [pl.BlockSpec((tm, d), lambda i: (i, 0)),
                _const_spec((1, d)), _const_spec(w_dq.shape), _const_spec((1, Q_LORA)),
                _const_spec(w_nope.shape), _const_spec(w_rope.shape), _const_spec(w_rope_swapped.shape),
                pl.BlockSpec((tm, N_HEADS * ROPE_DIM), lambda i: (i % n_pos, 0)),
                pl.BlockSpec((tm, N_HEADS * ROPE_DIM), lambda i: (i % n_pos, 0))]
    args = [x, gain.reshape(1, d), w_dq, gain_q.reshape(1, Q_LORA), w_nope, w_rope, w_rope_swapped, cs, sn]
    if absorbed:
        in_specs.append(_const_spec(w_uk_t.shape))
        args.append(w_uk_t)
        widths = (KV_LORA, ROPE_DIM)
    else:
        widths = (HEAD_DIM + ROPE_DIM,)
    return pl.pallas_call(
        functools.partial(_mla_q_kernel, absorbed=absorbed),
        grid=(m // tm,),
        in_specs=in_specs,
        out_specs=[pl.BlockSpec((N_HEADS, tm, w), lambda i: (0, i, 0)) for w in widths],
        out_shape=[jax.ShapeDtypeStruct((N_HEADS, m, w), BF16) for w in widths],
        compiler_params=_params("parallel"),
        name="mla_q",
    )(*args)


def _kv_up_kernel(c_ref, kr_ref, wuk_ref, wuv_ref, k_ref, v_ref):
    c = c_ref[...]
    k_nope = _dot(c, wuk_ref[...]).astype(BF16)
    v = _dot(c, wuv_ref[...]).astype(BF16)
    kr = kr_ref[...]
    for hd in range(N_HEADS):
        k_ref[hd, :, 0:HEAD_DIM] = k_nope[:, hd * HEAD_DIM:(hd + 1) * HEAD_DIM]
        k_ref[hd, :, HEAD_DIM:HEAD_DIM + ROPE_DIM] = kr
        v_ref[hd] = v[:, hd * HEAD_DIM:(hd + 1) * HEAD_DIM]


def _kv_up(cb, krb, w_uk, w_uv, tm):
    m = cb.shape[0]
    return pl.pallas_call(
        _kv_up_kernel,
        grid=(m // tm,),
        in_specs=[pl.BlockSpec((tm, KV_LORA), lambda i: (i, 0)),
                  pl.BlockSpec((tm, ROPE_DIM), lambda i: (i, 0)),
                  _const_spec(w_uk.shape), _const_spec(w_uv.shape)],
        out_specs=[pl.BlockSpec((N_HEADS, tm, HEAD_DIM + ROPE_DIM), lambda i: (0, i, 0)),
                   pl.BlockSpec((N_HEADS, tm, HEAD_DIM), lambda i: (0, i, 0))],
        out_shape=[jax.ShapeDtypeStruct((N_HEADS, m, HEAD_DIM + ROPE_DIM), BF16),
                   jax.ShapeDtypeStruct((N_HEADS, m, HEAD_DIM), BF16)],
        compiler_params=_params("parallel"),
        name="kv_up",
    )(cb, krb, w_uk, w_uv)


def _attn_prompt_kernel(q_ref, k_ref, v_ref, o_ref, s_scr, m_scr, l_scr, acc_scr):
    i = pl.program_id(2)
    m_scr[...] = jnp.full_like(m_scr, MASK_VALUE)
    l_scr[...] = jnp.zeros_like(l_scr)
    acc_scr[...] = jnp.zeros_like(acc_scr)

    def key_start(j):
        return j * ATTN_TK if isinstance(j, int) else pl.multiple_of(j * ATTN_TK, ATTN_TK)

    def scores(j, slot):
        for hd in range(ATTN_HEADS):
            s_scr[slot, hd] = _dot_nt(q_ref[hd], k_ref[hd, pl.ds(key_start(j), ATTN_TK), :])

    def softmax_pv(j, slot, masked):
        start = key_start(j)
        for hd in range(ATTN_HEADS):
            s = s_scr[slot, hd]
            if masked:
                q_pos = i * ATTN_TQ + lax.broadcasted_iota(jnp.int32, s.shape, 0)
                k_pos = start + lax.broadcasted_iota(jnp.int32, s.shape, 1)
                s = jnp.where(k_pos <= q_pos, s, MASK_VALUE)
            tiles = [s[:, t:t + LANES] for t in range(0, ATTN_TK, LANES)]
            tile_max = functools.reduce(jnp.maximum, tiles)
            m_prev = m_scr[hd]
            m_new = jnp.maximum(m_prev, jnp.max(tile_max, axis=-1, keepdims=True))
            alpha = jnp.exp2((m_prev - m_new) * ATTN_SCALE_LOG2)
            ps = [jnp.exp2((t - m_new) * ATTN_SCALE_LOG2) for t in tiles]
            l_scr[hd] = alpha * l_scr[hd] + functools.reduce(jnp.add, ps)
            pv = _dot(jnp.concatenate(ps, axis=1).astype(BF16), v_ref[hd, pl.ds(start, ATTN_TK), :])
            acc_scr[hd] = alpha * acc_scr[hd] + pv
            m_scr[hd] = m_new

    def pair(jj, carry):
        j = 2 * jj
        scores(j + 1, 1)
        softmax_pv(j, 0, False)
        scores(j + 2, 0)
        softmax_pv(j + 1, 1, False)
        return carry

    scores(0, 0)
    lax.fori_loop(0, i // 2, pair, 0)

    @pl.when(i % 2 == 1)
    def _():
        scores(i, 1)
        softmax_pv(i - 1, 0, False)
        softmax_pv(i, 1, True)

    @pl.when(i % 2 == 0)
    def _():
        softmax_pv(i, 0, True)

    for hd in range(ATTN_HEADS):
        inv_l = 1.0 / jnp.sum(l_scr[hd], axis=-1, keepdims=True)
        o_ref[:, hd * HEAD_DIM:(hd + 1) * HEAD_DIM] = (acc_scr[hd] * inv_l).astype(BF16)


def _attn_prompt(q, k, v, batch, seq):
    assert ATTN_TQ == ATTN_TK
    nq = seq // ATTN_TQ
    width = q.shape[-1]
    return pl.pallas_call(
        _attn_prompt_kernel,
        grid=(batch, N_HEADS // ATTN_HEADS, nq),
        in_specs=[pl.BlockSpec((ATTN_HEADS, ATTN_TQ, width), lambda b, h, i: (h, b * nq + i, 0)),
                  pl.BlockSpec((ATTN_HEADS, seq, width), lambda b, h, i: (h, b, 0)),
                  pl.BlockSpec((ATTN_HEADS, seq, HEAD_DIM), lambda b, h, i: (h, b, 0))],
        out_specs=pl.BlockSpec((ATTN_TQ, ATTN_HEADS * HEAD_DIM), lambda b, h, i: (b * nq + i, h)),
        out_shape=jax.ShapeDtypeStruct((batch * seq, N_HEADS * HEAD_DIM), BF16),
        scratch_shapes=[pltpu.VMEM((2, ATTN_HEADS, ATTN_TQ, ATTN_TK), F32),
                        pltpu.VMEM((ATTN_HEADS, ATTN_TQ, LANES), F32),
                        pltpu.VMEM((ATTN_HEADS, ATTN_TQ, LANES), F32),
                        pltpu.VMEM((ATTN_HEADS, ATTN_TQ, HEAD_DIM), F32)],
        compiler_params=_params("parallel", "parallel", "arbitrary"),
        name="attn_prompt",
    )(q, k, v)


def _attn_sample_kernel(pt_ref, ql_ref, qr_ref, cn_ref, krn_ref, cache_c, cache_krt, o_ref,
                        c_buf, krt_buf, key_scr, s_scr, sem):
    b = pl.program_id(0)
    n_seq = pl.num_programs(0)
    n_pages, page = c_buf.shape[1], c_buf.shape[2]
    slot = b % 2

    def page_copies(seq, to_slot, p):
        pg = pt_ref[seq, p]
        return (pltpu.make_async_copy(cache_c.at[pg], c_buf.at[to_slot, p], sem.at[to_slot, 0]),
                pltpu.make_async_copy(cache_krt.at[pg], krt_buf.at[to_slot, p], sem.at[to_slot, 1]))

    def for_all_pages(seq, to_slot, action):
        def body(p, carry):
            for cp in page_copies(seq, to_slot, p):
                action(cp)
            return carry
        lax.fori_loop(0, n_pages, body, 0, unroll=8)

    @pl.when(b == 0)
    def _():
        for_all_pages(0, 0, lambda cp: cp.start())

    @pl.when(b + 1 < n_seq)
    def _():
        for_all_pages(b + 1, 1 - slot, lambda cp: cp.start())

    for_all_pages(b, slot, lambda cp: cp.wait())

    ql = ql_ref[0]
    qr = qr_ref[0]
    keys = ATTN_SAMPLE_PAGES * page
    for j in range(n_pages // ATTN_SAMPLE_PAGES):
        first = j * ATTN_SAMPLE_PAGES
        kc = c_buf[slot, first:first + ATTN_SAMPLE_PAGES].reshape(keys, KV_LORA).astype(BF16)
        key_scr[j * keys:(j + 1) * keys, :] = kc
        krt = jnp.concatenate([krt_buf[slot, first + p].astype(BF16) for p in range(ATTN_SAMPLE_PAGES)],
                              axis=1)
        s_scr[:, j * keys:(j + 1) * keys] = _dot_nt(ql, kc) + _dot(qr, krt)

    cn = cn_ref[0].astype(BF16).astype(F32)
    krn = krn_ref[0].astype(BF16).astype(F32)
    s_new = (jnp.sum(ql.astype(F32) * cn, axis=-1, keepdims=True)
             + jnp.sum(qr.astype(F32) * krn, axis=-1, keepdims=True))
    s = s_scr[...]
    m = jnp.maximum(s_new, jnp.max(s, axis=-1, keepdims=True))
    p = jnp.exp2((s - m) * ATTN_SCALE_LOG2)
    p_new = jnp.exp2((s_new - m) * ATTN_SCALE_LOG2)
    l = p_new + jnp.sum(p, axis=-1, keepdims=True)
    acc = p_new * cn + _dot(p.astype(BF16), key_scr[...])
    o_ref[0] = acc / l


def _attn_sample(ql, qr, c_new, kr_new, cache_c, cache_krt, page_table):
    n, n_pages = page_table.shape
    page = cache_c.shape[1]
    assert n_pages % ATTN_SAMPLE_PAGES == 0
    grid_spec = pltpu.PrefetchScalarGridSpec(
        num_scalar_prefetch=1,
        grid=(n,),
        in_specs=[pl.BlockSpec((1, N_HEADS, KV_LORA), lambda b, pt: (b, 0, 0)),
                  pl.BlockSpec((1, N_HEADS, ROPE_DIM), lambda b, pt: (b, 0, 0)),
                  pl.BlockSpec((1, 1, KV_LORA), lambda b, pt: (b, 0, 0)),
                  pl.BlockSpec((1, 1, ROPE_DIM), lambda b, pt: (b, 0, 0)),
                  pl.BlockSpec(memory_space=pl.ANY), pl.BlockSpec(memory_space=pl.ANY)],
        out_specs=pl.BlockSpec((1, N_HEADS, KV_LORA), lambda b, pt: (b, 0, 0)),
        scratch_shapes=[pltpu.VMEM((2, n_pages, page, KV_LORA), F32),
                        pltpu.VMEM((2, n_pages, ROPE_DIM, page), F32),
                        pltpu.VMEM((n_pages * page, KV_LORA), BF16),
                        pltpu.VMEM((N_HEADS, n_pages * page), F32),
                        pltpu.SemaphoreType.DMA((2, 2))],
    )
    return pl.pallas_call(
        _attn_sample_kernel,
        grid_spec=grid_spec,
        out_shape=jax.ShapeDtypeStruct((n, N_HEADS, KV_LORA), F32),
        compiler_params=_params("arbitrary"),
        name="attn_sample",
    )(page_table, ql, qr, c_new.reshape(n, 1, KV_LORA), kr_new.reshape(n, 1, ROPE_DIM), cache_c, cache_krt)


def _mla_out_kernel(ol_ref, wuv_ref, wout_ref, g_ref, x_ref, o_ref, o_scr):
    for hd in range(N_HEADS):
        o_scr[:, hd * HEAD_DIM:(hd + 1) * HEAD_DIM] = _dot(ol_ref[hd], wuv_ref[hd]).astype(BF16)
    y = _dot(o_scr[...], wout_ref[...])
    o_ref[...] = x_ref[...] + _rms(y, g_ref[...])


def _mla_out(o_lat, w_uv, w_out, gain, x, tm):
    m, d = x.shape
    return pl.pallas_call(
        _mla_out_kernel,
        grid=(m // tm,),
        in_specs=[pl.BlockSpec((N_HEADS, tm, KV_LORA), lambda i: (0, i, 0)),
                  _const_spec(w_uv.shape), _const_spec(w_out.shape), _const_spec((1, d)),
                  pl.BlockSpec((tm, d), lambda i: (i, 0))],
        out_specs=pl.BlockSpec((tm, d), lambda i: (i, 0)),
        out_shape=jax.ShapeDtypeStruct((m, d), F32),
        scratch_shapes=[pltpu.VMEM((tm, d), BF16)],
        compiler_params=_params("parallel"),
        name="mla_out",
    )(o_lat, w_uv, w_out, gain.reshape(1, d), x)


def _rope_tables(pos):
    half = ROPE_DIM // 2
    inv = ROPE_THETA ** (-jnp.arange(half, dtype=F32) / half)
    ang = pos.astype(F32)[:, None] * inv[None, :]
    cos, sin = jnp.cos(ang), jnp.sin(ang)
    return jnp.concatenate([cos, cos], axis=-1), jnp.concatenate([-sin, sin], axis=-1)


def _swap_halves(w):
    half = w.shape[-1] // 2
    return jnp.concatenate([w[..., half:], w[..., :half]], axis=-1)


def kernel(x_prompt, x_sample, state_hgrn, cache_kv_latent, cache_k_rope, page_table, norm_gains, w_ffn_in, w_ffn_out, w_in_a, lb_logits, g_norm_a, w_out_a, kv_norm, w_dkv, kv_a_norm, w_ukv, w_dq, q_a_norm, w_uq, w_out_b):
    batch, seq, d = x_prompt.shape
    n_s = x_sample.shape[0]
    depth = norm_gains.shape[0]
    n_hgrn = w_in_a.shape[0]
    past_len = page_table.shape[1] * cache_kv_latent.shape[1]
    tm_p, tm_s = 512, n_s

    w_ffn_in_b = w_ffn_in.astype(BF16)
    w_ffn_out_b = w_ffn_out.astype(BF16)
    w_in_a_b = w_in_a.astype(BF16)
    w_out_a_b = w_out_a.astype(BF16)
    w_out_b_b = w_out_b.astype(BF16)
    w_dq_b = w_dq.astype(BF16)
    w_c = w_dkv[:, :KV_LORA].astype(BF16)
    w_kr = w_dkv[:, KV_LORA:].astype(BF16)
    w_kr_sw = _swap_halves(w_kr)
    w_ukv_h = w_ukv.reshape(KV_LORA, N_HEADS, 2 * HEAD_DIM).astype(BF16)
    w_uk_t = w_ukv_h[:, :, :HEAD_DIM].transpose(1, 2, 0)
    w_uv = w_ukv_h[:, :, HEAD_DIM:].transpose(1, 0, 2)
    w_uk_all = w_ukv_h[:, :, :HEAD_DIM].reshape(KV_LORA, N_HEADS * HEAD_DIM)
    w_uv_all = w_ukv_h[:, :, HEAD_DIM:].reshape(KV_LORA, N_HEADS * HEAD_DIM)
    w_uq_h = w_uq.reshape(-1, Q_LORA, N_HEADS, HEAD_DIM + ROPE_DIM).astype(BF16)
    n_mla = w_uq_h.shape[0]
    w_q_nope = w_uq_h[..., :HEAD_DIM].reshape(n_mla, Q_LORA, N_HEADS * HEAD_DIM)
    w_q_rope = w_uq_h[..., HEAD_DIM:].reshape(n_mla, Q_LORA, N_HEADS * ROPE_DIM)
    w_q_rope_sw = _swap_halves(w_uq_h[..., HEAD_DIM:]).reshape(n_mla, Q_LORA, N_HEADS * ROPE_DIM)
    cache_krt = jnp.swapaxes(cache_k_rope, 1, 2)

    cs_p, sn_p = _rope_tables(jnp.arange(seq, dtype=jnp.int32))
    cs_s, sn_s = _rope_tables(jnp.full((n_s,), past_len, dtype=jnp.int32))
    cs_p8, sn_p8, cs_s8, sn_s8 = (jnp.tile(t, (1, N_HEADS)) for t in (cs_p, sn_p, cs_s, sn_s))

    xp = x_prompt.reshape(batch * seq, d)
    xs = x_sample.reshape(n_s, d)
    states_p, zs_all, states_s = [], [], None
    kv_p = kv_s = None
    for l in range(depth):
        g = norm_gains[l]
        if l < n_hgrn:
            zp = _norm_mm(xp, g[0], w_in_a_b[l], tm_p)
            zs = _norm_mm(xs, g[0], w_in_a_b[l], tm_s)
            op, st_p = _hgrn_prompt(zp, lb_logits, g_norm_a[l], l, batch, seq)
            zs_all.append(zs)
            last = l == n_hgrn - 1
            os_, states_s = _hgrn_sample(zs_all if last else [zs], state_hgrn, lb_logits, g_norm_a[l], l, last)
            states_p.append(st_p)
            xp = _mm_post(op, w_out_a_b[l], g[1], xp, tm_p)
            xs = _mm_post(os_.astype(BF16), w_out_a_b[l], g[1], xs, tm_s)
        else:
            j = l - n_hgrn
            if kv_p is None:
                kv_p = _kv_shared(xp, kv_norm, w_c, w_kr, w_kr_sw, kv_a_norm, cs_p, sn_p, tm_p)
                kv_s = _kv_shared(xs, kv_norm, w_c, w_kr, w_kr_sw, kv_a_norm, cs_s, sn_s, tm_s)
                k_p, v_p = _kv_up(kv_p[2], kv_p[3], w_uk_all, w_uv_all, tm_p)
            q_args = (g[0], w_dq_b[j], q_a_norm[j], w_q_nope[j], w_q_rope[j], w_q_rope_sw[j])
            (q_p,) = _mla_q(xp, *q_args, cs_p8, sn_p8, tm_p)
            ql_s, qr_s = _mla_q(xs, *q_args, cs_s8, sn_s8, tm_s, w_uk_t=w_uk_t)
            o_p = _attn_prompt(q_p, k_p, v_p, batch, seq)
            ol_s = _attn_sample(ql_s.transpose(1, 0, 2), qr_s.transpose(1, 0, 2), kv_s[0], kv_s[1],
                                cache_kv_latent, cache_krt, page_table)
            xp = _mm_post(o_p, w_out_b_b[j], g[1], xp, tm_p)
            xs = _mla_out(ol_s.astype(BF16).transpose(1, 0, 2), w_uv, w_out_b_b[j], g[1], xs, tm_s)
        xp = _ffn(xp, g[2], g[3], w_ffn_in_b[l], w_ffn_out_b[l], tm_p)
        xs = _ffn(xs, g[2], g[3], w_ffn_in_b[l], w_ffn_out_b[l], tm_s)

    return (xp.reshape(batch, seq, d), xs.reshape(n_s, 1, d),
            jnp.stack(states_p), kv_p[0].reshape(batch, seq, KV_LORA), kv_p[1].reshape(batch, seq, ROPE_DIM),
            states_s, kv_s[0].reshape(n_s, 1, KV_LORA), kv_s[1].reshape(n_s, 1, ROPE_DIM))
```

```python
import functools

import jax
import jax.numpy as jnp
from jax import lax
from jax.experimental import pallas as pl
from jax.experimental.pallas import tpu as pltpu

F32 = jnp.float32
BF16 = jnp.bfloat16

D_MODEL = 1024
N_HEADS = 8
HEAD_DIM = 128
ROPE_DIM = 64
Q_LORA = 384
KV_LORA = 256
D_FF = 2816
N_HGRN = 2
EPS = 1e-6
LB_FLOOR = 1e-30
ROPE_THETA = 10000.0
ATTN_SCALE = (HEAD_DIM + ROPE_DIM) ** -0.5
ATTN_SCALE_LOG2 = ATTN_SCALE * 1.4426950408889634
MASK_VALUE = -1e30

HGRN_CHUNK = 128
FFN_COLS = 256
ATTN_TQ = 512
ATTN_TK = 512
ATTN_HEADS = 2
ATTN_SAMPLE_PAGES = 16
VMEM_LIMIT = 56 * 1024 * 1024
LANES = 128
SUBLANES = 8

_NT = (((1,), (1,)), ((), ()))
_TN = (((0,), (0,)), ((), ()))


def _params(*sem):
    return pltpu.CompilerParams(dimension_semantics=sem, vmem_limit_bytes=VMEM_LIMIT)


def _dot(a, b):
    return jnp.dot(a, b, preferred_element_type=F32)


def _dot_nt(a, b):
    return lax.dot_general(a, b, _NT, preferred_element_type=F32)


def _rms(xf, gain):
    ms = jnp.mean(xf * xf, axis=-1, keepdims=True)
    return xf * lax.rsqrt(ms + EPS) * gain


def _silu(x):
    return x * jax.nn.sigmoid(x)


def _const_spec(shape):
    n = len(shape)
    return pl.BlockSpec(shape, lambda *_: (0,) * n)


def _norm_mm_kernel(x_ref, g_ref, w_ref, o_ref, *, cols):
    h = _rms(x_ref[...], g_ref[...]).astype(BF16)
    for c in range(0, o_ref.shape[1], cols):
        o_ref[:, c:c + cols] = _dot(h, w_ref[:, c:c + cols])


def _layer_spec(w, layer):
    return pl.BlockSpec((None,) + w.shape[1:], lambda i: (layer, 0, 0))


def _norm_mm(x, gain, w, layer, tm):
    m, d = x.shape
    n = w.shape[2]
    return pl.pallas_call(
        functools.partial(_norm_mm_kernel, cols=1024),
        grid=(m // tm,),
        in_specs=[pl.BlockSpec((tm, d), lambda i: (i, 0)),
                  _const_spec((1, d)), _layer_spec(w, layer)],
        out_specs=pl.BlockSpec((tm, n), lambda i: (i, 0)),
        out_shape=jax.ShapeDtypeStruct((m, n), F32),
        compiler_params=_params("parallel"),
        name="norm_mm",
    )(x, gain.reshape(1, d), w)


def _ffn_kernel(*refs, with_mix):
    if with_mix:
        mix_ref, w_mix_ref, g_mix_ref = refs[:3]
        refs = refs[3:]
    x_ref, g_in_ref, g_out_ref, w_in_ref, w_out_ref, o_ref, a_scr = refs
    x = x_ref[...]
    if with_mix:
        x = x + _rms(_dot(mix_ref[...], w_mix_ref[...]), g_mix_ref[...])
    h = _rms(x, g_in_ref[...]).astype(BF16)
    for c in range(0, D_FF, FFN_COLS):
        g = _dot(h, w_in_ref[:, c:c + FFN_COLS])
        u = _dot(h, w_in_ref[:, D_FF + c:D_FF + c + FFN_COLS])
        a_scr[:, c:c + FFN_COLS] = (_silu(g) * u).astype(BF16)
    y = _dot(a_scr[...], w_out_ref[...])
    o_ref[...] = x + _rms(y, g_out_ref[...])


def _ffn(x, gains, w_in, w_out, layer, tm, mix=None, w_mix=None, mix_layer=None):
    m, d = x.shape
    with_mix = mix is not None
    in_specs = [pl.BlockSpec((tm, d), lambda i: (i, 0)),
                _const_spec((1, d)), _const_spec((1, d)),
                _layer_spec(w_in, layer), _layer_spec(w_out, layer)]
    args = [x, gains[2].reshape(1, d), gains[3].reshape(1, d), w_in, w_out]
    if with_mix:
        in_specs = [pl.BlockSpec((tm, mix.shape[1]), lambda i: (i, 0)),
                    _layer_spec(w_mix, mix_layer), _const_spec((1, d))] + in_specs
        args = [mix, w_mix, gains[1].reshape(1, d)] + args
    return pl.pallas_call(
        functools.partial(_ffn_kernel, with_mix=with_mix),
        grid=(m // tm,),
        in_specs=in_specs,
        out_specs=pl.BlockSpec((tm, d), lambda i: (i, 0)),
        out_shape=jax.ShapeDtypeStruct((m, d), F32),
        scratch_shapes=[pltpu.VMEM((tm, D_FF), BF16)],
        compiler_params=_params("parallel"),
        name="ffn",
    )(*args)


def _lower_bound(logits, layer):
    m = jnp.max(logits, axis=0, keepdims=True)
    e = jnp.exp(logits - m)
    p = e / jnp.sum(e, axis=0, keepdims=True)
    acc = p[0:1]
    for r in range(1, layer + 1):
        acc = acc + p[r:r + 1]
    return acc - p[0:1]


def _forget_gate(fz, lb):
    f = jnp.maximum(lb, LB_FLOOR) + (1.0 - lb) * jax.nn.sigmoid(fz)
    return jnp.minimum(f, 1.0)


def _mid_rows(b, m):
    c, d = b.shape
    if m >= SUBLANES:
        tiles = 2 * m // SUBLANES
        g4 = b.reshape(c // (2 * m), tiles, SUBLANES, d)
        src = g4[:, m // SUBLANES - 1, SUBLANES - 1:SUBLANES, :]
        tile = jnp.broadcast_to(src, (c // (2 * m), SUBLANES, d))
        return jnp.broadcast_to(tile[:, None], g4.shape).reshape(b.shape)
    g3 = b.reshape(c // SUBLANES, SUBLANES, d)
    if m == 4:
        return jnp.broadcast_to(g3[:, 3:4, :], g3.shape).reshape(b.shape)
    assert m == 2
    sub = lax.broadcasted_iota(jnp.int32, g3.shape, 1)
    out = jnp.where(sub < 4, jnp.broadcast_to(g3[:, 1:2, :], g3.shape),
                    jnp.broadcast_to(g3[:, 5:6, :], g3.shape))
    return out.reshape(b.shape)


def _hgrn_prompt_kernel(z_ref, lbl_ref, gn_ref, o_ref, st_ref, state_scr, *, layer):
    c = pl.program_id(1)
    chunk = z_ref.shape[0]

    @pl.when(c == 0)
    def _():
        state_scr[...] = jnp.zeros_like(state_scr)

    lb_all = _lower_bound(lbl_ref[...], layer)
    ti =lax.broadcasted_iota(jnp.int32, (chunk, chunk), 0)
    si = lax.broadcasted_iota(jnp.int32, (chunk, chunk), 1)
    level = jnp.where(ti > si, ti ^ si, 0)
    eye = ti == si
    tril = (ti >= si).astype(BF16)
    tril3 = jnp.concatenate([tril, tril, tril], axis=1)
    row = lax.broadcasted_iota(jnp.int32, (chunk, HEAD_DIM), 0)
    odd_row = (row & 1) != 0
    signs = []
    m = 1
    while m < chunk:
        signs.append((m, jnp.where((row & m) != 0, 1.0, -1.0)))
        m *= 2

    def intra_chunk(h):
        lo = h * HEAD_DIM
        hi = lo + HEAD_DIM
        qz = z_ref[:, lo:hi]
        fz = z_ref[:, D_MODEL + lo:D_MODEL + hi]
        f = _forget_gate(fz, lb_all[:, lo:hi])
        k = 1.0 - f
        q = _silu(qz) * (HEAD_DIM ** -0.5)
        g = jnp.log2(f)
        g_hi = g.astype(BF16)
        rest = g - g_hi.astype(F32)
        g_mid = rest.astype(BF16)
        g_lo = (rest - g_mid.astype(F32)).astype(BF16)
        b = _dot(tril3, jnp.concatenate([g_hi, g_mid, g_lo], axis=0))
        qb = q.astype(BF16)
        kb = k.astype(BF16)
        a = jnp.where(eye, _dot_nt(qb, kb), 0.0)
        for m, sign in signs:
            if m == 1:
                e = jnp.where(odd_row, f, 1.0).astype(BF16)
            else:
                e = jnp.exp2((b - _mid_rows(b, m)) * sign).astype(BF16)
            a = jnp.where(level >= m, _dot_nt(qb * e, kb * e), a)
        return q, k, b, a.astype(BF16)

    def read_out_and_update(h, q, k, b, a):
        lo = h * HEAD_DIM
        hi = lo + HEAD_DIM
        vb = z_ref[:, 2 * D_MODEL + lo:2 * D_MODEL + hi].astype(BF16)
        st = state_scr[h]
        o = _dot(a, vb) + _dot_nt((q * jnp.exp2(b)).astype(BF16), st.astype(BF16))
        b_last = b[chunk - 1:chunk, :]
        kd = (k * jnp.exp2(b_last - b)).astype(BF16)
        state_scr[h] = jnp.exp2(b_last) * st + lax.dot_general(vb, kd, _TN, preferred_element_type=F32)
        o = o * lax.rsqrt(jnp.mean(o * o, axis=-1, keepdims=True) + EPS)
        gz = z_ref[:, 3 * D_MODEL + lo:3 * D_MODEL + hi]
        o_ref[:, lo:hi] = (o * gn_ref[:, lo:hi] * _silu(gz)).astype(BF16)

    pending = intra_chunk(0)
    for h in range(N_HEADS):
        current = pending
        if h + 1 < N_HEADS:
            pending = intra_chunk(h + 1)
        read_out_and_update(h, *current)

    @pl.when(c == pl.num_programs(1) - 1)
    def _():
        for h in range(N_HEADS):
            st_ref[0, h] = state_scr[h].T


def _hgrn_prompt(z, lb_logits, g_norm, layer, batch, seq):
    n_chunks = seq // HGRN_CHUNK
    return pl.pallas_call(
        functools.partial(_hgrn_prompt_kernel, layer=layer),
        grid=(batch, n_chunks),
        in_specs=[pl.BlockSpec((HGRN_CHUNK, 4 * D_MODEL), lambda b, c: (b * n_chunks + c, 0)),
                  _const_spec(lb_logits.shape), _const_spec((1, D_MODEL))],
        out_specs=[pl.BlockSpec((HGRN_CHUNK, D_MODEL), lambda b, c: (b * n_chunks + c, 0)),
                   pl.BlockSpec((1, N_HEADS, HEAD_DIM, HEAD_DIM), lambda b, c: (b, 0, 0, 0))],
        out_shape=[jax.ShapeDtypeStruct((batch * seq, D_MODEL), BF16),
                   jax.ShapeDtypeStruct((batch, N_HEADS, HEAD_DIM, HEAD_DIM), F32)],
        scratch_shapes=[pltpu.VMEM((N_HEADS, HEAD_DIM, HEAD_DIM), F32)],
        compiler_params=_params("parallel", "arbitrary"),
        name="hgrn_prompt",
    )(z, lb_logits, g_norm.reshape(1, D_MODEL))


SAMPLE_GROUP = 4


def _hgrn_sample_kernel(*refs, layers, write_states):
    n = len(layers)
    z_refs = refs[:n]
    s_ref, lbl_ref, gn_ref, o_ref = refs[n:n + 4]
    so_ref = refs[n + 4] if write_states else None
    sb_scr, o_scr = refs[-2:]
    gn = gn_ref[...]
    for li, layer in enumerate(layers):
        z_ref = z_refs[li]
        lb = _lower_bound(lbl_ref[...], layer).reshape(N_HEADS, HEAD_DIM)
        z3 = z_ref[...].reshape(SAMPLE_GROUP * 4, N_HEADS, HEAD_DIM)
        f3 = _forget_gate(z3, lb[None])
        cols = f3.reshape(SAMPLE_GROUP * 4 * N_HEADS, HEAD_DIM).T
        mixes = li == n - 1
        for i in range(SAMPLE_GROUP):
            for h in range(N_HEADS):
                base = i * 4 * N_HEADS + N_HEADS + h
                f_col = cols[:, base:base + 1]
                v_row = z_ref[i, 2 * N_HEADS + h:2 * N_HEADS + h + 1, :]
                s_new = f_col * (s_ref[li, i, h] - v_row) + v_row
                if write_states:
                    so_ref[li, i, h] = s_new
                if mixes:
                    sb_scr[i * N_HEADS + h] = s_new.astype(BF16)
        if mixes:
            for i in range(SAMPLE_GROUP):
                q_rows = (_silu(z_ref[i, 0:N_HEADS, :]) * (HEAD_DIM ** -0.5)).astype(BF16)
                for h in range(N_HEADS):
                    r = i * N_HEADS + h
                    o_scr[r:r + 1, :] = _dot(q_rows, sb_scr[r])[h:h + 1, :]
            o = o_scr[...].reshape(SAMPLE_GROUP, N_HEADS, HEAD_DIM)
            o = o * lax.rsqrt(jnp.mean(o * o, axis=-1, keepdims=True) + EPS)
            o_ref[...] = o * gn[None] * _silu(z_ref[:, 3 * N_HEADS:4 * N_HEADS, :])


def _hgrn_sample(zs, state, lb_logits, g_norm, layer, write_states):
    n = zs[0].shape[0]
    layers = tuple(range(layer + 1)) if write_states else (layer,)
    assert len(zs) == len(layers)
    state_block = (len(layers), SAMPLE_GROUP, N_HEADS, HEAD_DIM, HEAD_DIM)
    first = layers[0]
    out_specs = [pl.BlockSpec((SAMPLE_GROUP, N_HEADS, HEAD_DIM), lambda i: (i, 0, 0))]
    out_shape = [jax.ShapeDtypeStruct((n, N_HEADS, HEAD_DIM), F32)]
    if write_states:
        out_specs.append(pl.BlockSpec(state_block, lambda i: (0, i, 0, 0, 0)))
        out_shape.append(jax.ShapeDtypeStruct((len(layers), n, N_HEADS, HEAD_DIM, HEAD_DIM), F32))
    outs = pl.pallas_call(
        functools.partial(_hgrn_sample_kernel, layers=layers, write_states=write_states),
        grid=(n // SAMPLE_GROUP,),
        in_specs=[pl.BlockSpec((SAMPLE_GROUP, 4 * N_HEADS, HEAD_DIM), lambda i: (i, 0, 0)) for _ in zs]
                 + [pl.BlockSpec(state_block, lambda i: (first, i, 0, 0, 0)),
                    _const_spec(lb_logits.shape), _const_spec((N_HEADS, HEAD_DIM))],
        out_specs=out_specs,
        out_shape=out_shape,
        scratch_shapes=[pltpu.VMEM((SAMPLE_GROUP * N_HEADS, HEAD_DIM, HEAD_DIM), BF16),
                        pltpu.VMEM((SAMPLE_GROUP * N_HEADS, HEAD_DIM), F32)],
        compiler_params=_params("parallel"),
        name="hgrn_sample",
    )(*[z.reshape(n, 4 * N_HEADS, HEAD_DIM) for z in zs], state, lb_logits, g_norm.reshape(N_HEADS, HEAD_DIM))
    o = outs[0].reshape(n, D_MODEL)
    return (o, outs[1]) if write_states else (o, None)


def _kv_kernel(x_ref, g_ref, wc_ref, wkr_ref, wkrs_ref, ga_ref, cs_ref, sn_ref,
               c_ref, kr_ref, cb_ref, krb_ref):
    h = _rms(x_ref[...], g_ref[...]).astype(BF16)
    c = _rms(_dot(h, wc_ref[...]), ga_ref[...])
    kr = _dot(h, wkr_ref[...]) * cs_ref[...] + _dot(h, wkrs_ref[...]) * sn_ref[...]
    c_ref[...] = c
    kr_ref[...] = kr
    cb_ref[...] = c.astype(BF16)
    krb_ref[...] = kr.astype(BF16)


def _kv_shared(x, gain, w_c, w_kr, w_kr_swapped, gain_a, cs, sn, tm):
    m, d = x.shape
    n_pos = cs.shape[0] // tm
    return pl.pallas_call(
        _kv_kernel,
        grid=(m // tm,),
        in_specs=[pl.BlockSpec((tm, d), lambda i: (i, 0)),
                  _const_spec((1, d)), _const_spec(w_c.shape), _const_spec(w_kr.shape),
                  _const_spec(w_kr_swapped.shape), _const_spec((1, KV_LORA)),
                  pl.BlockSpec((tm, ROPE_DIM), lambda i: (i % n_pos, 0)),
                  pl.BlockSpec((tm, ROPE_DIM), lambda i: (i % n_pos, 0))],
        out_specs=[pl.BlockSpec((tm, KV_LORA), lambda i: (i, 0)),
                   pl.BlockSpec((tm, ROPE_DIM), lambda i: (i, 0)),
                   pl.BlockSpec((tm, KV_LORA), lambda i: (i, 0)),
                   pl.BlockSpec((tm, ROPE_DIM), lambda i: (i, 0))],
        out_shape=[jax.ShapeDtypeStruct((m, KV_LORA), F32),
                   jax.ShapeDtypeStruct((m, ROPE_DIM), F32),
                   jax.ShapeDtypeStruct((m, KV_LORA), BF16),
                   jax.ShapeDtypeStruct((m, ROPE_DIM), BF16)],
        compiler_params=_params("parallel"),
        name="kv_shared",
    )(x, gain.reshape(1, d), w_c, w_kr, w_kr_swapped, gain_a.reshape(1, KV_LORA), cs, sn)


def _mla_q_kernel(x_ref, g_ref, wdq_ref, gq_ref, wn_ref, wr_ref, wrs_ref, cs_ref, sn_ref, *rest, absorbed):
    h = _rms(x_ref[...], g_ref[...]).astype(BF16)
    qa = _rms(_dot(h, wdq_ref[...]), gq_ref[...]).astype(BF16)
    q_nope = _dot(qa, wn_ref[...]).astype(BF16)
    q_rope = (_dot(qa, wr_ref[...]) * cs_ref[...] + _dot(qa, wrs_ref[...]) * sn_ref[...]).astype(BF16)
    for hd in range(N_HEADS):
        nope = q_nope[:, hd * HEAD_DIM:(hd + 1) * HEAD_DIM]
        rope = q_rope[:, hd * ROPE_DIM:(hd + 1) * ROPE_DIM]
        if absorbed:
            wukt_ref, ql_ref, qr_ref = rest
            ql_ref[hd] = _dot(nope, wukt_ref[hd]).astype(BF16)
            qr_ref[hd] = rope
        else:
            (q_ref,) = rest
            q_ref[hd, :, 0:HEAD_DIM] = nope
            q_ref[hd, :, HEAD_DIM:HEAD_DIM + ROPE_DIM] = rope


def _mla_q(x, gain, w_dq, gain_q, w_nope, w_rope, w_rope_swapped, cs, sn, tm, w_uk_t=None):
    m, d = x.shape
    n_pos = cs.shape[0] // tm
    absorbed = w_uk_t is not None
    in_specs = [pl.BlockSpec((tm, d), lambda i: (i, 0)),
                _const_spec((1, d)), _const_spec(w_dq.shape), _const_spec((1, Q_LORA)),
                _const_spec(w_nope.shape), _const_spec(w_rope.shape), _const_spec(w_rope_swapped.shape),
                pl.BlockSpec((tm, N_HEADS * ROPE_DIM), lambda i: (i % n_pos, 0)),
                pl.BlockSpec((tm, N_HEADS * ROPE_DIM), lambda i: (i % n_pos, 0))]
    args = [x, gain.reshape(1, d), w_dq, gain_q.reshape(1, Q_LORA), w_nope, w_rope, w_rope_swapped, cs, sn]
    if absorbed:
        in_specs.append(_const_spec(w_uk_t.shape))
        args.append(w_uk_t)
        widths = (KV_LORA, ROPE_DIM)
    else:
        widths = (HEAD_DIM + ROPE_DIM,)
    return pl.pallas_call(
        functools.partial(_mla_q_kernel, absorbed=absorbed),
        grid=(m // tm,),
        in_specs=in_specs,
        out_specs=[pl.BlockSpec((N_HEADS, tm, w), lambda i: (0, i, 0)) for w in widths],
        out_shape=[jax.ShapeDtypeStruct((N_HEADS, m, w), BF16) for w in widths],
        compiler_params=_params("parallel"),
        name="mla_q",
    )(*args)


def _kv_up_kernel(c_ref, kr_ref, wuk_ref, wuv_ref, k_ref, v_ref):
    c = c_ref[...]
    k_nope = _dot(c, wuk_ref[...]).astype(BF16)
    v = _dot(c, wuv_ref[...]).astype(BF16)
    kr = kr_ref[...]
    for hd in range(N_HEADS):
        k_ref[hd, :, 0:HEAD_DIM] = k_nope[:, hd * HEAD_DIM:(hd + 1) * HEAD_DIM]
        k_ref[hd, :, HEAD_DIM:HEAD_DIM + ROPE_DIM] = kr
        v_ref[hd] = v[:, hd * HEAD_DIM:(hd + 1) * HEAD_DIM]


def _kv_up(cb, krb, w_uk, w_uv, tm):
    m = cb.shape[0]
    return pl.pallas_call(
        _kv_up_kernel,
        grid=(m // tm,),
        in_specs=[pl.BlockSpec((tm, KV_LORA), lambda i: (i, 0)),
                  pl.BlockSpec((tm, ROPE_DIM), lambda i: (i, 0)),
                  _const_spec(w_uk.shape), _const_spec(w_uv.shape)],
        out_specs=[pl.BlockSpec((N_HEADS, tm, HEAD_DIM + ROPE_DIM), lambda i: (0, i, 0)),
                   pl.BlockSpec((N_HEADS, tm, HEAD_DIM), lambda i: (0, i, 0))],
        out_shape=[jax.ShapeDtypeStruct((N_HEADS, m, HEAD_DIM + ROPE_DIM), BF16),
                   jax.ShapeDtypeStruct((N_HEADS, m, HEAD_DIM), BF16)],
        compiler_params=_params("parallel"),
        name="kv_up",
    )(cb, krb, w_uk, w_uv)


def _attn_prompt_kernel(q_ref, k_ref, v_ref, o_ref, s_scr, m_scr, l_scr, acc_scr):
    i = pl.program_id(2)
    m_scr[...] = jnp.full_like(m_scr, MASK_VALUE)
    l_scr[...] = jnp.zeros_like(l_scr)
    acc_scr[...] = jnp.zeros_like(acc_scr)

    def key_start(j):
        return j * ATTN_TK if isinstance(j, int) else pl.multiple_of(j * ATTN_TK, ATTN_TK)

    def scores(j, slot):
        for hd in range(ATTN_HEADS):
            s_scr[slot, hd] = _dot_nt(q_ref[hd], k_ref[hd, pl.ds(key_start(j), ATTN_TK), :])

    def softmax_pv(j, slot, masked):
        start = key_start(j)
        for hd in range(ATTN_HEADS):
            s = s_scr[slot, hd]
            if masked:
                q_pos = i * ATTN_TQ + lax.broadcasted_iota(jnp.int32, s.shape, 0)
                k_pos = start + lax.broadcasted_iota(jnp.int32, s.shape, 1)
                s = jnp.where(k_pos <= q_pos, s, MASK_VALUE)
            tiles = [s[:, t:t + LANES] for t in range(0, ATTN_TK, LANES)]
            tile_max = functools.reduce(jnp.maximum, tiles)
            m_prev = m_scr[hd]
            m_new = jnp.maximum(m_prev, jnp.max(tile_max, axis=-1, keepdims=True))
            alpha = jnp.exp2((m_prev - m_new) * ATTN_SCALE_LOG2)
            ps = [jnp.exp2((t - m_new) * ATTN_SCALE_LOG2) for t in tiles]
            l_scr[hd] = alpha * l_scr[hd] + functools.reduce(jnp.add, ps)
            pv = _dot(jnp.concatenate(ps, axis=1).astype(BF16), v_ref[hd, pl.ds(start, ATTN_TK), :])
            acc_scr[hd] = alpha * acc_scr[hd] + pv
            m_scr[hd] = m_new

    def pair(jj, carry):
        j = 2 * jj
        scores(j + 1, 1)
        softmax_pv(j, 0, False)
        scores(j + 2, 0)
        softmax_pv(j + 1, 1, False)
        return carry

    scores(0, 0)
    lax.fori_loop(0, i // 2, pair, 0)

    @pl.when(i % 2 == 1)
    def _():
        scores(i, 1)
        softmax_pv(i - 1, 0, False)
        softmax_pv(i, 1, True)

    @pl.when(i % 2 == 0)
    def _():
        softmax_pv(i, 0, True)

    for hd in range(ATTN_HEADS):
        inv_l = 1.0 / jnp.sum(l_scr[hd], axis=-1, keepdims=True)
        o_ref[:, hd * HEAD_DIM:(hd + 1) * HEAD_DIM] = (acc_scr[hd] * inv_l).astype(BF16)


def _attn_prompt(q, k, v, batch, seq):
    assert ATTN_TQ == ATTN_TK
    nq = seq // ATTN_TQ
    width = q.shape[-1]
    return pl.pallas_call(
        _attn_prompt_kernel,
        grid=(batch, N_HEADS // ATTN_HEADS, nq),
        in_specs=[pl.BlockSpec((ATTN_HEADS, ATTN_TQ, width), lambda b, h, i: (h, b * nq + i, 0)),
                  pl.BlockSpec((ATTN_HEADS, seq, width), lambda b, h, i: (h, b, 0)),
                  pl.BlockSpec((ATTN_HEADS, seq, HEAD_DIM), lambda b, h, i: (h, b, 0))],
        out_specs=pl.BlockSpec((ATTN_TQ, ATTN_HEADS * HEAD_DIM), lambda b, h, i: (b * nq + i, h)),
        out_shape=jax.ShapeDtypeStruct((batch * seq, N_HEADS * HEAD_DIM), BF16),
        scratch_shapes=[pltpu.VMEM((2, ATTN_HEADS, ATTN_TQ, ATTN_TK), F32),
                        pltpu.VMEM((ATTN_HEADS, ATTN_TQ, LANES), F32),
                        pltpu.VMEM((ATTN_HEADS, ATTN_TQ, LANES), F32),
                        pltpu.VMEM((ATTN_HEADS, ATTN_TQ, HEAD_DIM), F32)],
        compiler_params=_params("parallel", "parallel", "arbitrary"),
        name="attn_prompt",
    )(q, k, v)


def _attn_sample_kernel(pt_ref, ql_ref, qr_ref, cn_ref, krn_ref, cache_c, cache_krt, o_ref,
                        c_buf, krt_buf, key_scr, s_scr, sem):
    b = pl.program_id(0)
    n_seq = pl.num_programs(0)
    n_pages, page = c_buf.shape[1], c_buf.shape[2]
    slot = b % 2

    def page_copies(seq, to_slot, p):
        pg = pt_ref[seq, p]
        return (pltpu.make_async_copy(cache_c.at[pg], c_buf.at[to_slot, p], sem.at[to_slot, 0]),
                pltpu.make_async_copy(cache_krt.at[pg], krt_buf.at[to_slot, p], sem.at[to_slot, 1]))

    def for_all_pages(seq, to_slot, action):
        def body(p, carry):
            for cp in page_copies(seq, to_slot, p):
                action(cp)
            return carry
        lax.fori_loop(0, n_pages, body, 0, unroll=8)

    @pl.when(b == 0)
    def _():
        for_all_pages(0, 0, lambda cp: cp.start())

    @pl.when(b + 1 < n_seq)
    def _():
        for_all_pages(b + 1, 1 - slot, lambda cp: cp.start())

    for_all_pages(b, slot, lambda cp: cp.wait())

    ql = ql_ref[0]
    qr = qr_ref[0]
    keys = ATTN_SAMPLE_PAGES * page
    for j in range(n_pages // ATTN_SAMPLE_PAGES):
        first = j * ATTN_SAMPLE_PAGES
        kc = c_buf[slot, first:first + ATTN_SAMPLE_PAGES].reshape(keys, KV_LORA).astype(BF16)
        key_scr[j * keys:(j + 1) * keys, :] = kc
        krt = jnp.concatenate([krt_buf[slot, first + p].astype(BF16) for p in range(ATTN_SAMPLE_PAGES)],
                              axis=1)
        s_scr[:, j * keys:(j + 1) * keys] = _dot_nt(ql, kc) + _dot(qr, krt)

    cn = cn_ref[0].astype(BF16).astype(F32)
    krn = krn_ref[0].astype(BF16).astype(F32)
    s_new = (jnp.sum(ql.astype(F32) * cn, axis=-1, keepdims=True)
             + jnp.sum(qr.astype(F32) * krn, axis=-1, keepdims=True))
    s = s_scr[...]
    m = jnp.maximum(s_new, jnp.max(s, axis=-1, keepdims=True))
    p = jnp.exp2((s - m) * ATTN_SCALE_LOG2)
    p_new = jnp.exp2((s_new - m) * ATTN_SCALE_LOG2)
    l = p_new + jnp.sum(p, axis=-1, keepdims=True)
    acc = p_new * cn + _dot(p.astype(BF16), key_scr[...])
    o_ref[0] = acc / l


def _attn_sample(ql, qr, c_new, kr_new, cache_c, cache_krt, page_table):
    n, n_pages = page_table.shape
    page = cache_c.shape[1]
    assert n_pages % ATTN_SAMPLE_PAGES == 0
    grid_spec = pltpu.PrefetchScalarGridSpec(
        num_scalar_prefetch=1,
        grid=(n,),
        in_specs=[pl.BlockSpec((1, N_HEADS, KV_LORA), lambda b, pt: (b, 0, 0)),
                  pl.BlockSpec((1, N_HEADS, ROPE_DIM), lambda b, pt: (b, 0, 0)),
                  pl.BlockSpec((1, 1, KV_LORA), lambda b, pt: (b, 0, 0)),
                  pl.BlockSpec((1, 1, ROPE_DIM), lambda b, pt: (b, 0, 0)),
                  pl.BlockSpec(memory_space=pl.ANY), pl.BlockSpec(memory_space=pl.ANY)],
        out_specs=pl.BlockSpec((1, N_HEADS, KV_LORA), lambda b, pt: (b, 0, 0)),
        scratch_shapes=[pltpu.VMEM((2, n_pages, page, KV_LORA), F32),
                        pltpu.VMEM((2, n_pages, ROPE_DIM, page), F32),
                        pltpu.VMEM((n_pages * page, KV_LORA), BF16),
                        pltpu.VMEM((N_HEADS, n_pages * page), F32),
                        pltpu.SemaphoreType.DMA((2, 2))],
    )
    return pl.pallas_call(
        _attn_sample_kernel,
        grid_spec=grid_spec,
        out_shape=jax.ShapeDtypeStruct((n, N_HEADS, KV_LORA), F32),
        compiler_params=_params("arbitrary"),
        name="attn_sample",
    )(page_table, ql, qr, c_new.reshape(n, 1, KV_LORA), kr_new.reshape(n, 1, ROPE_DIM), cache_c, cache_krt)


def _mla_out_kernel(ol_ref, wuv_ref, wout_ref, g_ref, x_ref, o_ref, o_scr):
    for hd in range(N_HEADS):
        o_scr[:, hd * HEAD_DIM:(hd + 1) * HEAD_DIM] = _dot(ol_ref[hd], wuv_ref[hd]).astype(BF16)
    y = _dot(o_scr[...], wout_ref[...])
    o_ref[...] = x_ref[...] + _rms(y, g_ref[...])


def _mla_out(o_lat, w_uv, w_out, gain, x, tm):
    m, d = x.shape
    return pl.pallas_call(
        _mla_out_kernel,
        grid=(m // tm,),
        in_specs=[pl.BlockSpec((N_HEADS, tm, KV_LORA), lambda i: (0, i, 0)),
                  _const_spec(w_uv.shape), _const_spec(w_out.shape), _const_spec((1, d)),
                  pl.BlockSpec((tm, d), lambda i: (i, 0))],
        out_specs=pl.BlockSpec((tm, d), lambda i: (i, 0)),
        out_shape=jax.ShapeDtypeStruct((m, d), F32),
        scratch_shapes=[pltpu.VMEM((tm, d), BF16)],
        compiler_params=_params("parallel"),
        name="mla_out",
    )(o_lat, w_uv, w_out, gain.reshape(1, d), x)


def _rope_tables(pos):
    half = ROPE_DIM // 2
    inv = ROPE_THETA ** (-jnp.arange(half, dtype=F32) / half)
    ang = pos.astype(F32)[:, None] * inv[None, :]
    cos, sin = jnp.cos(ang), jnp.sin(ang)
    return jnp.concatenate([cos, cos], axis=-1), jnp.concatenate([-sin, sin], axis=-1)


def _swap_halves(w):
    half = w.shape[-1] // 2
    return jnp.concatenate([w[..., half:], w[..., :half]], axis=-1)


def kernel(x_prompt, x_sample, state_hgrn, cache_kv_latent, cache_k_rope, page_table, norm_gains, w_ffn_in, w_ffn_out, w_in_a, lb_logits, g_norm_a, w_out_a, kv_norm, w_dkv, kv_a_norm, w_ukv, w_dq, q_a_norm, w_uq, w_out_b):
    batch, seq, d = x_prompt.shape
    n_s = x_sample.shape[0]
    depth = norm_gains.shape[0]
    n_hgrn = w_in_a.shape[0]
    past_len = page_table.shape[1] * cache_kv_latent.shape[1]
    tm_p, tm_s = 512, n_s

    w_ffn_in_b = w_ffn_in.astype(BF16)
    w_ffn_out_b = w_ffn_out.astype(BF16)
    w_in_a_b = w_in_a.astype(BF16)
    w_out_a_b = w_out_a.astype(BF16)
    w_out_b_b = w_out_b.astype(BF16)
    w_dq_b = w_dq.astype(BF16)
    w_c = w_dkv[:, :KV_LORA].astype(BF16)
    w_kr = w_dkv[:, KV_LORA:].astype(BF16)
    w_kr_sw = _swap_halves(w_kr)
    w_ukv_h = w_ukv.reshape(KV_LORA, N_HEADS, 2 * HEAD_DIM).astype(BF16)
    w_uk_t = w_ukv_h[:, :, :HEAD_DIM].transpose(1, 2, 0)
    w_uv = w_ukv_h[:, :, HEAD_DIM:].transpose(1, 0, 2)
    w_uk_all = w_ukv_h[:, :, :HEAD_DIM].reshape(KV_LORA, N_HEADS * HEAD_DIM)
    w_uv_all = w_ukv_h[:, :, HEAD_DIM:].reshape(KV_LORA, N_HEADS * HEAD_DIM)
    w_uq_h = w_uq.reshape(-1, Q_LORA, N_HEADS, HEAD_DIM + ROPE_DIM).astype(BF16)
    n_mla = w_uq_h.shape[0]
    w_q_nope = w_uq_h[..., :HEAD_DIM].reshape(n_mla, Q_LORA, N_HEADS * HEAD_DIM)
    w_q_rope = w_uq_h[..., HEAD_DIM:].reshape(n_mla, Q_LORA, N_HEADS * ROPE_DIM)
    w_q_rope_sw = _swap_halves(w_uq_h[..., HEAD_DIM:]).reshape(n_mla, Q_LORA, N_HEADS * ROPE_DIM)
    cache_krt = jnp.swapaxes(cache_k_rope, 1, 2)

    cs_p, sn_p = _rope_tables(jnp.arange(seq, dtype=jnp.int32))
    cs_s, sn_s = _rope_tables(jnp.full((n_s,), past_len, dtype=jnp.int32))
    cs_p8, sn_p8, cs_s8, sn_s8 = (jnp.tile(t, (1, N_HEADS)) for t in (cs_p, sn_p, cs_s, sn_s))

    xp = x_prompt.reshape(batch * seq, d)
    xs = x_sample.reshape(n_s, d)
    states_p, zs_all, states_s = [], [], None
    kv_p = kv_s = None
    for l in range(depth):
        g = norm_gains[l]
        if l < n_hgrn:
            zp = _norm_mm(xp, g[0], w_in_a_b, l, tm_p)
            zs = _norm_mm(xs, g[0], w_in_a_b, l, tm_s)
            op, st_p = _hgrn_prompt(zp, lb_logits, g_norm_a[l], l, batch, seq)
            zs_all.append(zs)
            last = l == n_hgrn - 1
            os_, states_s = _hgrn_sample(zs_all if last else [zs], state_hgrn, lb_logits, g_norm_a[l], l, last)
            states_p.append(st_p)
            mix_p = dict(mix=op, w_mix=w_out_a_b, mix_layer=l)
            mix_s = dict(mix=os_.astype(BF16), w_mix=w_out_a_b, mix_layer=l)
        else:
            j = l - n_hgrn
            if kv_p is None:
                kv_p = _kv_shared(xp, kv_norm, w_c, w_kr, w_kr_sw, kv_a_norm, cs_p, sn_p, tm_p)
                kv_s = _kv_shared(xs, kv_norm, w_c, w_kr, w_kr_sw, kv_a_norm, cs_s, sn_s, tm_s)
                k_p, v_p = _kv_up(kv_p[2], kv_p[3], w_uk_all, w_uv_all, tm_p)
            q_args = (g[0], w_dq_b[j], q_a_norm[j], w_q_nope[j], w_q_rope[j], w_q_rope_sw[j])
            (q_p,) = _mla_q(xp, *q_args, cs_p8, sn_p8, tm_p)
            ql_s, qr_s = _mla_q(xs, *q_args, cs_s8, sn_s8, tm_s, w_uk_t=w_uk_t)
            o_p = _attn_prompt(q_p, k_p, v_p, batch, seq)
            ol_s = _attn_sample(ql_s.transpose(1, 0, 2), qr_s.transpose(1, 0, 2), kv_s[0], kv_s[1],
                                cache_kv_latent, cache_krt, page_table)
            mix_p = dict(mix=o_p, w_mix=w_out_b_b, mix_layer=j)
            xs = _mla_out(ol_s.astype(BF16).transpose(1, 0, 2), w_uv, w_out_b_b[j], g[1], xs, tm_s)
            mix_s = {}
        xp = _ffn(xp, g, w_ffn_in_b, w_ffn_out_b, l, tm_p, **mix_p)
        xs = _ffn(xs, g, w_ffn_in_b, w_ffn_out_b, l, tm_s, **mix_s)

    return (xp.reshape(batch, seq, d), xs.reshape(n_s, 1, d),
            jnp.stack(states_p), kv_p[0].reshape(batch, seq, KV_LORA), kv_p[1].reshape(batch, seq, ROPE_DIM),
            states_s, kv_s[0].reshape(n_s, 1, KV_LORA), kv_s[1].reshape(n_s, 1, ROPE_DIM))
```

```python
import functools

import jax
import jax.numpy as jnp
from jax import lax
from jax.experimental import pallas as pl
from jax.experimental.pallas import tpu as pltpu

F32 = jnp.float32
BF16 = jnp.bfloat16

D_MODEL = 1024
N_HEADS = 8
HEAD_DIM = 128
ROPE_DIM = 64
Q_LORA = 384
KV_LORA = 256
D_FF = 2816
N_HGRN = 2
EPS = 1e-6
LB_FLOOR = 1e-30
ROPE_THETA = 10000.0
ATTN_SCALE = (HEAD_DIM + ROPE_DIM) ** -0.5
ATTN_SCALE_LOG2 = ATTN_SCALE * 1.4426950408889634
MASK_VALUE = -1e30

HGRN_CHUNK = 128
FFN_COLS = 256
ATTN_TQ = 512
ATTN_TK = 512
ATTN_HEADS = 4
ATTN_SAMPLE_PAGES = 16
VMEM_LIMIT = 56 * 1024 * 1024
LANES = 128
SUBLANES = 8

_NT = (((1,), (1,)), ((), ()))
_TN = (((0,), (0,)), ((), ()))


def _params(*sem):
    return pltpu.CompilerParams(dimension_semantics=sem, vmem_limit_bytes=VMEM_LIMIT)


def _dot(a, b):
    return jnp.dot(a, b, preferred_element_type=F32)


def _dot_nt(a, b):
    return lax.dot_general(a, b, _NT, preferred_element_type=F32)


def _rms(xf, gain):
    ms = jnp.mean(xf * xf, axis=-1, keepdims=True)
    return xf * lax.rsqrt(ms + EPS) * gain


def _silu(x):
    return x * jax.nn.sigmoid(x)


def _const_spec(shape):
    n = len(shape)
    return pl.BlockSpec(shape, lambda *_: (0,) * n)


def _norm_mm_kernel(x_ref, g_ref, w_ref, o_ref, *, cols):
    h = _rms(x_ref[...], g_ref[...]).astype(BF16)
    for c in range(0, o_ref.shape[1], cols):
        o_ref[:, c:c + cols] = _dot(h, w_ref[:, c:c + cols])


def _layer_spec(w, layer):
    return pl.BlockSpec((None,) + w.shape[1:], lambda i: (layer, 0, 0))


def _norm_mm(x, gain, w, layer, tm):
    m, d = x.shape
    n = w.shape[2]
    return pl.pallas_call(
        functools.partial(_norm_mm_kernel, cols=1024),
        grid=(m // tm,),
        in_specs=[pl.BlockSpec((tm, d), lambda i: (i, 0)),
                  _const_spec((1, d)), _layer_spec(w, layer)],
        out_specs=pl.BlockSpec((tm, n), lambda i: (i, 0)),
        out_shape=jax.ShapeDtypeStruct((m, n), F32),
        compiler_params=_params("parallel"),
        name="norm_mm",
    )(x, gain.reshape(1, d), w)


def _ffn_kernel(*refs, with_mix):
    if with_mix:
        mix_ref, w_mix_ref, g_mix_ref = refs[:3]
        refs = refs[3:]
    x_ref, g_in_ref, g_out_ref, w_in_ref, w_out_ref, o_ref, a_scr = refs
    x = x_ref[...]
    if with_mix:
        x = x + _rms(_dot(mix_ref[...], w_mix_ref[...]), g_mix_ref[...])
    h = _rms(x, g_in_ref[...]).astype(BF16)
    for c in range(0, D_FF, FFN_COLS):
        g = _dot(h, w_in_ref[:, c:c + FFN_COLS])
        u = _dot(h, w_in_ref[:, D_FF + c:D_FF + c + FFN_COLS])
        a_scr[:, c:c + FFN_COLS] = (_silu(g) * u).astype(BF16)
    y = _dot(a_scr[...], w_out_ref[...])
    o_ref[...] = x + _rms(y, g_out_ref[...])


def _ffn(x, gains, w_in, w_out, layer, tm, mix=None, w_mix=None, mix_layer=None):
    m, d = x.shape
    with_mix = mix is not None
    in_specs = [pl.BlockSpec((tm, d), lambda i: (i, 0)),
                _const_spec((1, d)), _const_spec((1, d)),
                _layer_spec(w_in, layer), _layer_spec(w_out, layer)]
    args = [x, gains[2].reshape(1, d), gains[3].reshape(1, d), w_in, w_out]
    if with_mix:
        in_specs = [pl.BlockSpec((tm, mix.shape[1]), lambda i: (i, 0)),
                    _layer_spec(w_mix, mix_layer), _const_spec((1, d))] + in_specs
        args = [mix, w_mix, gains[1].reshape(1, d)] + args
    return pl.pallas_call(
        functools.partial(_ffn_kernel, with_mix=with_mix),
        grid=(m // tm,),
        in_specs=in_specs,
        out_specs=pl.BlockSpec((tm, d), lambda i: (i, 0)),
        out_shape=jax.ShapeDtypeStruct((m, d), F32),
        scratch_shapes=[pltpu.VMEM((tm, D_FF), BF16)],
        compiler_params=_params("parallel"),
        name="ffn",
    )(*args)


def _lower_bound(logits, layer):
    m = jnp.max(logits, axis=0, keepdims=True)
    e = jnp.exp(logits - m)
    p = e / jnp.sum(e, axis=0, keepdims=True)
    acc = p[0:1]
    for r in range(1, layer + 1):
        acc = acc + p[r:r + 1]
    return acc - p[0:1]


def _forget_gate(fz, lb):
    f = jnp.maximum(lb, LB_FLOOR) + (1.0 - lb) * jax.nn.sigmoid(fz)
    return jnp.minimum(f, 1.0)


def _mid_rows(b, m):
    c, d = b.shape
    if m >= SUBLANES:
        tiles = 2 * m // SUBLANES
        g4 = b.reshape(c // (2 * m), tiles, SUBLANES, d)
        src = g4[:, m // SUBLANES - 1, SUBLANES - 1:SUBLANES, :]
        tile = jnp.broadcast_to(src, (c // (2 * m), SUBLANES, d))
        return jnp.broadcast_to(tile[:, None], g4.shape).reshape(b.shape)
    g3 = b.reshape(c // SUBLANES, SUBLANES, d)
    if m == 4:
        return jnp.broadcast_to(g3[:, 3:4, :], g3.shape).reshape(b.shape)
    assert m == 2
    sub = lax.broadcasted_iota(jnp.int32, g3.shape, 1)
    out = jnp.where(sub < 4, jnp.broadcast_to(g3[:, 1:2, :], g3.shape),
                    jnp.broadcast_to(g3[:, 5:6, :], g3.shape))
    return out.reshape(b.shape)


def _hgrn_prompt_kernel(z_ref, lbl_ref, gn_ref, o_ref, st_ref, state_scr, *, layer):
    c = pl.program_id(1)
    chunk = z_ref.shape[0]

    @pl.when(c == 0)
    def _():
        state_scr[...] = jnp.zeros_like(state_scr)

    lb_all = _lower_bound(lbl_ref[...], layer)
    ti =lax.broadcasted_iota(jnp.int32, (chunk, chunk), 0)
    si = lax.broadcasted_iota(jnp.int32, (chunk, chunk), 1)
    level = jnp.where(ti > si, ti ^ si, 0)
    eye = ti == si
    tril = (ti >= si).astype(BF16)
    tril3 = jnp.concatenate([tril, tril, tril], axis=1)
    row = lax.broadcasted_iota(jnp.int32, (chunk, HEAD_DIM), 0)
    odd_row = (row & 1) != 0
    signs = []
    m = 1
    while m < chunk:
        signs.append((m, jnp.where((row & m) != 0, 1.0, -1.0)))
        m *= 2

    def intra_chunk(h):
        lo = h * HEAD_DIM
        hi = lo + HEAD_DIM
        qz = z_ref[:, lo:hi]
        fz = z_ref[:, D_MODEL + lo:D_MODEL + hi]
        f = _forget_gate(fz, lb_all[:, lo:hi])
        k = 1.0 - f
        q = _silu(qz) * (HEAD_DIM ** -0.5)
        g = jnp.log2(f)
        g_hi = g.astype(BF16)
        rest = g - g_hi.astype(F32)
        g_mid = rest.astype(BF16)
        g_lo = (rest - g_mid.astype(F32)).astype(BF16)
        b = _dot(tril3, jnp.concatenate([g_hi, g_mid, g_lo], axis=0))
        qb = q.astype(BF16)
        kb = k.astype(BF16)
        a = jnp.where(eye, _dot_nt(qb, kb), 0.0)
        for m, sign in signs:
            if m == 1:
                e = jnp.where(odd_row, f, 1.0).astype(BF16)
            else:
                e = jnp.exp2((b - _mid_rows(b, m)) * sign).astype(BF16)
            a = jnp.where(level >= m, _dot_nt(qb * e, kb * e), a)
        return q, k, b, a.astype(BF16)

    def read_out_and_update(h, q, k, b, a):
        lo = h * HEAD_DIM
        hi = lo + HEAD_DIM
        vb = z_ref[:, 2 * D_MODEL + lo:2 * D_MODEL + hi].astype(BF16)
        st = state_scr[h]
        o = _dot(a, vb) + _dot_nt((q * jnp.exp2(b)).astype(BF16), st.astype(BF16))
        b_last = b[chunk - 1:chunk, :]
        kd = (k * jnp.exp2(b_last - b)).astype(BF16)
        state_scr[h] = jnp.exp2(b_last) * st + lax.dot_general(vb, kd, _TN, preferred_element_type=F32)
        o = o * lax.rsqrt(jnp.mean(o * o, axis=-1, keepdims=True) + EPS)
        gz = z_ref[:, 3 * D_MODEL + lo:3 * D_MODEL + hi]
        o_ref[:, lo:hi] = (o * gn_ref[:, lo:hi] * _silu(gz)).astype(BF16)

    pending = intra_chunk(0)
    for h in range(N_HEADS):
        current = pending
        if h + 1 < N_HEADS:
            pending = intra_chunk(h + 1)
        read_out_and_update(h, *current)

    @pl.when(c == pl.num_programs(1) - 1)
    def _():
        for h in range(N_HEADS):
            st_ref[0, h] = state_scr[h].T


def _hgrn_prompt(z, lb_logits, g_norm, layer, batch, seq):
    n_chunks = seq // HGRN_CHUNK
    return pl.pallas_call(
        functools.partial(_hgrn_prompt_kernel, layer=layer),
        grid=(batch, n_chunks),
        in_specs=[pl.BlockSpec((HGRN_CHUNK, 4 * D_MODEL), lambda b, c: (b * n_chunks + c, 0)),
                  _const_spec(lb_logits.shape), _const_spec((1, D_MODEL))],
        out_specs=[pl.BlockSpec((HGRN_CHUNK, D_MODEL), lambda b, c: (b * n_chunks + c, 0)),
                   pl.BlockSpec((1, N_HEADS, HEAD_DIM, HEAD_DIM), lambda b, c: (b, 0, 0, 0))],
        out_shape=[jax.ShapeDtypeStruct((batch * seq, D_MODEL), BF16),
                   jax.ShapeDtypeStruct((batch, N_HEADS, HEAD_DIM, HEAD_DIM), F32)],
        scratch_shapes=[pltpu.VMEM((N_HEADS, HEAD_DIM, HEAD_DIM), F32)],
        compiler_params=_params("parallel", "arbitrary"),
        name="hgrn_prompt",
    )(z, lb_logits, g_norm.reshape(1, D_MODEL))


SAMPLE_GROUP = 4


def _hgrn_sample_kernel(*refs, layers, write_states):
    n = len(layers)
    z_refs = refs[:n]
    s_ref, lbl_ref, gn_ref, o_ref = refs[n:n + 4]
    so_ref = refs[n + 4] if write_states else None
    sb_scr, o_scr = refs[-2:]
    gn = gn_ref[...]
    for li, layer in enumerate(layers):
        z_ref = z_refs[li]
        lb = _lower_bound(lbl_ref[...], layer).reshape(N_HEADS, HEAD_DIM)
        z3 = z_ref[...].reshape(SAMPLE_GROUP * 4, N_HEADS, HEAD_DIM)
        f3 = _forget_gate(z3, lb[None])
        cols = f3.reshape(SAMPLE_GROUP * 4 * N_HEADS, HEAD_DIM).T
        mixes = li == n - 1
        for i in range(SAMPLE_GROUP):
            for h in range(N_HEADS):
                base = i * 4 * N_HEADS + N_HEADS + h
                f_col = cols[:, base:base + 1]
                v_row = z_ref[i, 2 * N_HEADS + h:2 * N_HEADS + h + 1, :]
                s_new = f_col * (s_ref[li, i, h] - v_row) + v_row
                if write_states:
                    so_ref[li, i, h] = s_new
                if mixes:
                    sb_scr[i * N_HEADS + h] = s_new.astype(BF16)
        if mixes:
            for i in range(SAMPLE_GROUP):
                q_rows = (_silu(z_ref[i, 0:N_HEADS, :]) * (HEAD_DIM ** -0.5)).astype(BF16)
                for h in range(N_HEADS):
                    r = i * N_HEADS + h
                    o_scr[r:r + 1, :] = _dot(q_rows, sb_scr[r])[h:h + 1, :]
            o = o_scr[...].reshape(SAMPLE_GROUP, N_HEADS, HEAD_DIM)
            o = o * lax.rsqrt(jnp.mean(o * o, axis=-1, keepdims=True) + EPS)
            o_ref[...] = o * gn[None] * _silu(z_ref[:, 3 * N_HEADS:4 * N_HEADS, :])


def _hgrn_sample(zs, state, lb_logits, g_norm, layer, write_states):
    n = zs[0].shape[0]
    layers = tuple(range(layer + 1)) if write_states else (layer,)
    assert len(zs) == len(layers)
    state_block = (len(layers), SAMPLE_GROUP, N_HEADS, HEAD_DIM, HEAD_DIM)
    first = layers[0]
    out_specs = [pl.BlockSpec((SAMPLE_GROUP, N_HEADS, HEAD_DIM), lambda i: (i, 0, 0))]
    out_shape = [jax.ShapeDtypeStruct((n, N_HEADS, HEAD_DIM), F32)]
    if write_states:
        out_specs.append(pl.BlockSpec(state_block, lambda i: (0, i, 0, 0, 0)))
        out_shape.append(jax.ShapeDtypeStruct((len(layers), n, N_HEADS, HEAD_DIM, HEAD_DIM), F32))
    outs = pl.pallas_call(
        functools.partial(_hgrn_sample_kernel, layers=layers, write_states=write_states),
        grid=(n // SAMPLE_GROUP,),
        in_specs=[pl.BlockSpec((SAMPLE_GROUP, 4 * N_HEADS, HEAD_DIM), lambda i: (i, 0, 0)) for _ in zs]
                 + [pl.BlockSpec(state_block, lambda i: (first, i, 0, 0, 0)),
                    _const_spec(lb_logits.shape), _const_spec((N_HEADS, HEAD_DIM))],
        out_specs=out_specs,
        out_shape=out_shape,
        scratch_shapes=[pltpu.VMEM((SAMPLE_GROUP * N_HEADS, HEAD_DIM, HEAD_DIM), BF16),
                        pltpu.VMEM((SAMPLE_GROUP * N_HEADS, HEAD_DIM), F32)],
        compiler_params=_params("parallel"),
        name="hgrn_sample",
    )(*[z.reshape(n, 4 * N_HEADS, HEAD_DIM) for z in zs], state, lb_logits, g_norm.reshape(N_HEADS, HEAD_DIM))
    o = outs[0].reshape(n, D_MODEL)
    return (o, outs[1]) if write_states else (o, None)


def _kv_kernel(x_ref, g_ref, wc_ref, wkr_ref, wkrs_ref, ga_ref, cs_ref, sn_ref, *rest, per_head):
    h = _rms(x_ref[...], g_ref[...]).astype(BF16)
    c = _rms(_dot(h, wc_ref[...]), ga_ref[...])
    kr = _dot(h, wkr_ref[...]) * cs_ref[...] + _dot(h, wkrs_ref[...]) * sn_ref[...]
    if not per_head:
        c_ref, kr_ref = rest
    else:
        wuk_ref, wuv_ref, c_ref, kr_ref, k_ref, v_ref = rest
        cb = c.astype(BF16)
        krb = kr.astype(BF16)
        k_nope = _dot(cb, wuk_ref[...]).astype(BF16)
        v = _dot(cb, wuv_ref[...]).astype(BF16)
        for hd in range(N_HEADS):
            k_ref[hd, :, 0:HEAD_DIM] = k_nope[:, hd * HEAD_DIM:(hd + 1) * HEAD_DIM]
            k_ref[hd, :, HEAD_DIM:HEAD_DIM + ROPE_DIM] = krb
            v_ref[hd] = v[:, hd * HEAD_DIM:(hd + 1) * HEAD_DIM]
    c_ref[...] = c
    kr_ref[...] = kr


def _kv_shared(x, gain, w_c, w_kr, w_kr_swapped, gain_a, cs, sn, tm, w_uk=None, w_uv=None):
    m, d = x.shape
    n_pos = cs.shape[0] // tm
    per_head = w_uk is not None
    in_specs = [pl.BlockSpec((tm, d), lambda i: (i, 0)),
                _const_spec((1, d)), _const_spec(w_c.shape), _const_spec(w_kr.shape),
                _const_spec(w_kr_swapped.shape), _const_spec((1, KV_LORA)),
                pl.BlockSpec((tm, ROPE_DIM), lambda i: (i % n_pos, 0)),
                pl.BlockSpec((tm, ROPE_DIM), lambda i: (i % n_pos, 0))]
    args = [x, gain.reshape(1, d), w_c, w_kr, w_kr_swapped, gain_a.reshape(1, KV_LORA), cs, sn]
    out_specs = [pl.BlockSpec((tm, KV_LORA), lambda i: (i, 0)),
                 pl.BlockSpec((tm, ROPE_DIM), lambda i: (i, 0))]
    out_shape = [jax.ShapeDtypeStruct((m, KV_LORA), F32), jax.ShapeDtypeStruct((m, ROPE_DIM), F32)]
    if per_head:
        in_specs += [_const_spec(w_uk.shape), _const_spec(w_uv.shape)]
        args += [w_uk, w_uv]
        out_specs += [pl.BlockSpec((N_HEADS, tm, HEAD_DIM + ROPE_DIM), lambda i: (0, i, 0)),
                      pl.BlockSpec((N_HEADS, tm, HEAD_DIM), lambda i: (0, i, 0))]
        out_shape += [jax.ShapeDtypeStruct((N_HEADS, m, HEAD_DIM + ROPE_DIM), BF16),
                      jax.ShapeDtypeStruct((N_HEADS, m, HEAD_DIM), BF16)]
    return pl.pallas_call(
        functools.partial(_kv_kernel, per_head=per_head),
        grid=(m // tm,),
        in_specs=in_specs,
        out_specs=out_specs,
        out_shape=out_shape,
        compiler_params=_params("parallel"),
        name="kv_shared",
    )(*args)


def _mla_q_kernel(x_ref, g_ref, wdq_ref, gq_ref, wn_ref, wr_ref, wrs_ref, cs_ref, sn_ref, *rest, absorbed):
    h = _rms(x_ref[...], g_ref[...]).astype(BF16)
    qa = _rms(_dot(h, wdq_ref[...]), gq_ref[...]).astype(BF16)
    q_nope = _dot(qa, wn_ref[...]).astype(BF16)
    reps = wr_ref.shape[1] // LANES
    cs = jnp.concatenate([cs_ref[...]] * reps, axis=1)
    sn = jnp.concatenate([sn_ref[...]] * reps, axis=1)
    q_rope = (_dot(qa, wr_ref[...]) * cs + _dot(qa, wrs_ref[...]) * sn).astype(BF16)
    for hd in range(N_HEADS):
        nope = q_nope[:, hd * HEAD_DIM:(hd + 1) * HEAD_DIM]
        rope = q_rope[:, hd * ROPE_DIM:(hd + 1) * ROPE_DIM]
        if absorbed:
            wukt_ref, ql_ref, qr_ref = rest
            ql_ref[hd] = _dot(nope, wukt_ref[hd]).astype(BF16)
            qr_ref[hd] = rope
        else:
            (q_ref,) = rest
            q_ref[hd, :, 0:HEAD_DIM] = nope
            q_ref[hd, :, HEAD_DIM:HEAD_DIM + ROPE_DIM] = rope


def _mla_q(x, gain, w_dq, gain_q, w_nope, w_rope, w_rope_swapped, cs, sn, tm, w_uk_t=None):
    m, d = x.shape
    n_pos = cs.shape[0] // tm
    absorbed = w_uk_t is not None
    in_specs = [pl.BlockSpec((tm, d), lambda i: (i, 0)),
                _const_spec((1, d)), _const_spec(w_dq.shape), _const_spec((1, Q_LORA)),
                _const_spec(w_nope.shape), _const_spec(w_rope.shape), _const_spec(w_rope_swapped.shape),
                pl.BlockSpec((tm, LANES), lambda i: (i % n_pos, 0)),
                pl.BlockSpec((tm, LANES), lambda i: (i % n_pos, 0))]
    args = [x, gain.reshape(1, d), w_dq, gain_q.reshape(1, Q_LORA), w_nope, w_rope, w_rope_swapped, cs, sn]
    if absorbed:
        in_specs.append(_const_spec(w_uk_t.shape))
        args.append(w_uk_t)
        widths = (KV_LORA, ROPE_DIM)
    else:
        widths = (HEAD_DIM + ROPE_DIM,)
    return pl.pallas_call(
        functools.partial(_mla_q_kernel, absorbed=absorbed),
        grid=(m // tm,),
        in_specs=in_specs,
        out_specs=[pl.BlockSpec((N_HEADS, tm, w), lambda i: (0, i, 0)) for w in widths],
        out_shape=[jax.ShapeDtypeStruct((N_HEADS, m, w), BF16) for w in widths],
        compiler_params=_params("parallel"),
        name="mla_q",
    )(*args)


def _attn_prompt_kernel(q_ref, k_ref, v_ref, o_ref, s_scr, m_scr, l_scr, acc_scr):
    i = pl.program_id(2)
    m_scr[...] = jnp.full_like(m_scr, MASK_VALUE)
    l_scr[...] = jnp.zeros_like(l_scr)
    acc_scr[...] = jnp.zeros_like(acc_scr)

    def key_start(j):
        return j * ATTN_TK if isinstance(j, int) else pl.multiple_of(j * ATTN_TK, ATTN_TK)

    def scores(j, slot):
        for hd in range(ATTN_HEADS):
            s_scr[slot, hd] = _dot_nt(q_ref[hd], k_ref[hd, pl.ds(key_start(j), ATTN_TK), :])

    def softmax_pv(j, slot, masked):
        start = key_start(j)
        for hd in range(ATTN_HEADS):
            s = s_scr[slot, hd]
            if masked:
                q_pos = i * ATTN_TQ + lax.broadcasted_iota(jnp.int32, s.shape, 0)
                k_pos = start + lax.broadcasted_iota(jnp.int32, s.shape, 1)
                s = jnp.where(k_pos <= q_pos, s, MASK_VALUE)
            tiles = [s[:, t:t + LANES] for t in range(0, ATTN_TK, LANES)]
            tile_max = functools.reduce(jnp.maximum, tiles)
            m_prev = m_scr[hd]
            m_new = jnp.maximum(m_prev, jnp.max(tile_max, axis=-1, keepdims=True))
            alpha = jnp.exp2((m_prev - m_new) * ATTN_SCALE_LOG2)
            ps = [jnp.exp2((t - m_new) * ATTN_SCALE_LOG2) for t in tiles]
            l_scr[hd] = alpha * l_scr[hd] + functools.reduce(jnp.add, ps)
            pv = _dot(jnp.concatenate(ps, axis=1).astype(BF16), v_ref[hd, pl.ds(start, ATTN_TK), :])
            acc_scr[hd] = alpha * acc_scr[hd] + pv
            m_scr[hd] = m_new

    def pair(jj, carry):
        j = 2 * jj
        scores(j + 1, 1)
        softmax_pv(j, 0, False)
        scores(j + 2, 0)
        softmax_pv(j + 1, 1, False)
        return carry

    scores(0, 0)
    lax.fori_loop(0, i // 2, pair, 0)

    @pl.when(i % 2 == 1)
    def _():
        scores(i, 1)
        softmax_pv(i - 1, 0, False)
        softmax_pv(i, 1, True)

    @pl.when(i % 2 == 0)
    def _():
        softmax_pv(i, 0, True)

    for hd in range(ATTN_HEADS):
        inv_l = 1.0 / jnp.sum(l_scr[hd], axis=-1, keepdims=True)
        o_ref[:, hd * HEAD_DIM:(hd + 1) * HEAD_DIM] = (acc_scr[hd] * inv_l).astype(BF16)


def _attn_prompt(q, k, v, batch, seq):
    assert ATTN_TQ == ATTN_TK
    nq = seq // ATTN_TQ
    width = q.shape[-1]
    return pl.pallas_call(
        _attn_prompt_kernel,
        grid=(batch, N_HEADS // ATTN_HEADS, nq),
        in_specs=[pl.BlockSpec((ATTN_HEADS, ATTN_TQ, width), lambda b, h, i: (h, b * nq + i, 0)),
                  pl.BlockSpec((ATTN_HEADS, seq, width), lambda b, h, i: (h, b, 0)),
                  pl.BlockSpec((ATTN_HEADS, seq, HEAD_DIM), lambda b, h, i: (h, b, 0))],
        out_specs=pl.BlockSpec((ATTN_TQ, ATTN_HEADS * HEAD_DIM), lambda b, h, i: (b * nq + i, h)),
        out_shape=jax.ShapeDtypeStruct((batch * seq, N_HEADS * HEAD_DIM), BF16),
        scratch_shapes=[pltpu.VMEM((2, ATTN_HEADS, ATTN_TQ, ATTN_TK), F32),
                        pltpu.VMEM((ATTN_HEADS, ATTN_TQ, LANES), F32),
                        pltpu.VMEM((ATTN_HEADS, ATTN_TQ, LANES), F32),
                        pltpu.VMEM((ATTN_HEADS, ATTN_TQ, HEAD_DIM), F32)],
        compiler_params=_params("parallel", "parallel", "arbitrary"),
        name="attn_prompt",
    )(q, k, v)


def _attn_sample_kernel(pt_ref, ql_ref, qr_ref, cn_ref, krn_ref, cache_c, cache_krt, o_ref,
                        c_buf, krt_buf, key_scr, s_scr, sem):
    b = pl.program_id(0)
    n_seq = pl.num_programs(0)
    n_pages, page = c_buf.shape[1], c_buf.shape[2]
    slot = b % 2

    def page_copies(seq, to_slot, p):
        pg = pt_ref[seq, p]
        return (pltpu.make_async_copy(cache_c.at[pg], c_buf.at[to_slot, p], sem.at[to_slot, 0]),
                pltpu.make_async_copy(cache_krt.at[pg], krt_buf.at[to_slot, p], sem.at[to_slot, 1]))

    def for_all_pages(seq, to_slot, action):
        def body(p, carry):
            for cp in page_copies(seq, to_slot, p):
                action(cp)
            return carry
        lax.fori_loop(0, n_pages, body, 0, unroll=8)

    @pl.when(b == 0)
    def _():
        for_all_pages(0, 0, lambda cp: cp.start())

    @pl.when(b + 1 < n_seq)
    def _():
        for_all_pages(b + 1, 1 - slot, lambda cp: cp.start())

    for_all_pages(b, slot, lambda cp: cp.wait())

    ql = ql_ref[0]
    qr = qr_ref[0]
    keys = ATTN_SAMPLE_PAGES * page
    for j in range(n_pages // ATTN_SAMPLE_PAGES):
        first = j * ATTN_SAMPLE_PAGES
        kc = c_buf[slot, first:first + ATTN_SAMPLE_PAGES].reshape(keys, KV_LORA).astype(BF16)
        key_scr[j * keys:(j + 1) * keys, :] = kc
        krt = jnp.concatenate([krt_buf[slot, first + p].astype(BF16) for p in range(ATTN_SAMPLE_PAGES)],
                              axis=1)
        s_scr[:, j * keys:(j + 1) * keys] = _dot_nt(ql, kc) + _dot(qr, krt)

    cn = cn_ref[0].astype(BF16).astype(F32)
    krn = krn_ref[0].astype(BF16).astype(F32)
    s_new = (jnp.sum(ql.astype(F32) * cn, axis=-1, keepdims=True)
             + jnp.sum(qr.astype(F32) * krn, axis=-1, keepdims=True))
    s = s_scr[...]
    m = jnp.maximum(s_new, jnp.max(s, axis=-1, keepdims=True))
    p = jnp.exp2((s - m) * ATTN_SCALE_LOG2)
    p_new = jnp.exp2((s_new - m) * ATTN_SCALE_LOG2)
    l = p_new + jnp.sum(p, axis=-1, keepdims=True)
    acc = p_new * cn + _dot(p.astype(BF16), key_scr[...])
    o_ref[0] = acc / l


def _attn_sample(ql, qr, c_new, kr_new, cache_c, cache_krt, page_table):
    n, n_pages = page_table.shape
    page = cache_c.shape[1]
    assert n_pages % ATTN_SAMPLE_PAGES == 0
    grid_spec = pltpu.PrefetchScalarGridSpec(
        num_scalar_prefetch=1,
        grid=(n,),
        in_specs=[pl.BlockSpec((1, N_HEADS, KV_LORA), lambda b, pt: (b, 0, 0)),
                  pl.BlockSpec((1, N_HEADS, ROPE_DIM), lambda b, pt: (b, 0, 0)),
                  pl.BlockSpec((1, 1, KV_LORA), lambda b, pt: (b, 0, 0)),
                  pl.BlockSpec((1, 1, ROPE_DIM), lambda b, pt: (b, 0, 0)),
                  pl.BlockSpec(memory_space=pl.ANY), pl.BlockSpec(memory_space=pl.ANY)],
        out_specs=pl.BlockSpec((1, N_HEADS, KV_LORA), lambda b, pt: (b, 0, 0)),
        scratch_shapes=[pltpu.VMEM((2, n_pages, page, KV_LORA), F32),
                        pltpu.VMEM((2, n_pages, ROPE_DIM, page), F32),
                        pltpu.VMEM((n_pages * page, KV_LORA), BF16),
                        pltpu.VMEM((N_HEADS, n_pages * page), F32),
                        pltpu.SemaphoreType.DMA((2, 2))],
    )
    return pl.pallas_call(
        _attn_sample_kernel,
        grid_spec=grid_spec,
        out_shape=jax.ShapeDtypeStruct((n, N_HEADS, KV_LORA), F32),
        compiler_params=_params("arbitrary"),
        name="attn_sample",
    )(page_table, ql, qr, c_new.reshape(n, 1, KV_LORA), kr_new.reshape(n, 1, ROPE_DIM), cache_c, cache_krt)


def _mla_out_kernel(ol_ref, wuv_ref, wout_ref, g_ref, x_ref, o_ref, o_scr):
    for hd in range(N_HEADS):
        o_scr[:, hd * HEAD_DIM:(hd + 1) * HEAD_DIM] = _dot(ol_ref[hd], wuv_ref[hd]).astype(BF16)
    y = _dot(o_scr[...], wout_ref[...])
    o_ref[...] = x_ref[...] + _rms(y, g_ref[...])


def _mla_out(o_lat, w_uv, w_out, gain, x, tm):
    m, d = x.shape
    return pl.pallas_call(
        _mla_out_kernel,
        grid=(m // tm,),
        in_specs=[pl.BlockSpec((N_HEADS, tm, KV_LORA), lambda i: (0, i, 0)),
                  _const_spec(w_uv.shape), _const_spec(w_out.shape), _const_spec((1, d)),
                  pl.BlockSpec((tm, d), lambda i: (i, 0))],
        out_specs=pl.BlockSpec((tm, d), lambda i: (i, 0)),
        out_shape=jax.ShapeDtypeStruct((m, d), F32),
        scratch_shapes=[pltpu.VMEM((tm, d), BF16)],
        compiler_params=_params("parallel"),
        name="mla_out",
    )(o_lat, w_uv, w_out, gain.reshape(1, d), x)


def _rope_tables(pos):
    half = ROPE_DIM // 2
    inv = ROPE_THETA ** (-jnp.arange(half, dtype=F32) / half)
    ang = pos.astype(F32)[:, None] * inv[None, :]
    cos, sin = jnp.cos(ang), jnp.sin(ang)
    return jnp.concatenate([cos, cos], axis=-1), jnp.concatenate([-sin, sin], axis=-1)


def _swap_halves(w):
    half = w.shape[-1] // 2
    return jnp.concatenate([w[..., half:], w[..., :half]], axis=-1)


def kernel(x_prompt, x_sample, state_hgrn, cache_kv_latent, cache_k_rope, page_table, norm_gains, w_ffn_in, w_ffn_out, w_in_a, lb_logits, g_norm_a, w_out_a, kv_norm, w_dkv, kv_a_norm, w_ukv, w_dq, q_a_norm, w_uq, w_out_b):
    batch, seq, d = x_prompt.shape
    n_s = x_sample.shape[0]
    depth = norm_gains.shape[0]
    n_hgrn = w_in_a.shape[0]
    past_len = page_table.shape[1] * cache_kv_latent.shape[1]
    tm_p, tm_s = 512, n_s

    w_ffn_in_b = w_ffn_in.astype(BF16)
    w_ffn_out_b = w_ffn_out.astype(BF16)
    w_in_a_b = w_in_a.astype(BF16)
    w_out_a_b = w_out_a.astype(BF16)
    w_out_b_b = w_out_b.astype(BF16)
    w_dq_b = w_dq.astype(BF16)
    w_c = w_dkv[:, :KV_LORA].astype(BF16)
    w_kr = w_dkv[:, KV_LORA:].astype(BF16)
    w_kr_sw = _swap_halves(w_kr)
    w_ukv_h = w_ukv.reshape(KV_LORA, N_HEADS, 2 * HEAD_DIM).astype(BF16)
    w_uk_t = w_ukv_h[:, :, :HEAD_DIM].transpose(1, 2, 0)
    w_uv = w_ukv_h[:, :, HEAD_DIM:].transpose(1, 0, 2)
    w_uk_all = w_ukv_h[:, :, :HEAD_DIM].reshape(KV_LORA, N_HEADS * HEAD_DIM)
    w_uv_all = w_ukv_h[:, :, HEAD_DIM:].reshape(KV_LORA, N_HEADS * HEAD_DIM)
    w_uq_h = w_uq.reshape(-1, Q_LORA, N_HEADS, HEAD_DIM + ROPE_DIM).astype(BF16)
    n_mla = w_uq_h.shape[0]
    w_q_nope = w_uq_h[..., :HEAD_DIM].reshape(n_mla, Q_LORA, N_HEADS * HEAD_DIM)
    w_q_rope = w_uq_h[..., HEAD_DIM:].reshape(n_mla, Q_LORA, N_HEADS * ROPE_DIM)
    w_q_rope_sw = _swap_halves(w_uq_h[..., HEAD_DIM:]).reshape(n_mla, Q_LORA, N_HEADS * ROPE_DIM)
    cache_krt = jnp.swapaxes(cache_k_rope, 1, 2)

    cs_p, sn_p = _rope_tables(jnp.arange(seq, dtype=jnp.int32))
    cs_s, sn_s = _rope_tables(jnp.full((n_s,), past_len, dtype=jnp.int32))
    cs_p8, sn_p8, cs_s8, sn_s8 = (jnp.tile(t, (1, LANES // ROPE_DIM)) for t in (cs_p, sn_p, cs_s, sn_s))

    xp = x_prompt.reshape(batch * seq, d)
    xs = x_sample.reshape(n_s, d)
    states_p, zs_all, states_s = [], [], None
    kv_p = kv_s = None
    for l in range(depth):
        g = norm_gains[l]
        if l < n_hgrn:
            zp = _norm_mm(xp, g[0], w_in_a_b, l, tm_p)
            zs = _norm_mm(xs, g[0], w_in_a_b, l, tm_s)
            op, st_p = _hgrn_prompt(zp, lb_logits, g_norm_a[l], l, batch, seq)
            zs_all.append(zs)
            last = l == n_hgrn - 1
            os_, states_s = _hgrn_sample(zs_all if last else [zs], state_hgrn, lb_logits, g_norm_a[l], l, last)
            states_p.append(st_p)
            mix_p = dict(mix=op, w_mix=w_out_a_b, mix_layer=l)
            mix_s = dict(mix=os_.astype(BF16), w_mix=w_out_a_b, mix_layer=l)
        else:
            j = l - n_hgrn
            if kv_p is None:
                kv_p = _kv_shared(xp, kv_norm, w_c, w_kr, w_kr_sw, kv_a_norm, cs_p, sn_p, tm_p,
                                  w_uk=w_uk_all, w_uv=w_uv_all)
                kv_s = _kv_shared(xs, kv_norm, w_c, w_kr, w_kr_sw, kv_a_norm, cs_s, sn_s, tm_s)
                k_p, v_p = kv_p[2], kv_p[3]
            q_args = (g[0], w_dq_b[j], q_a_norm[j], w_q_nope[j], w_q_rope[j], w_q_rope_sw[j])
            (q_p,) = _mla_q(xp, *q_args, cs_p8, sn_p8, tm_p)
            ql_s, qr_s = _mla_q(xs, *q_args, cs_s8, sn_s8, tm_s, w_uk_t=w_uk_t)
            o_p = _attn_prompt(q_p, k_p, v_p, batch, seq)
            ol_s = _attn_sample(ql_s.transpose(1, 0, 2), qr_s.transpose(1, 0, 2), kv_s[0], kv_s[1],
                                cache_kv_latent, cache_krt, page_table)
            mix_p = dict(mix=o_p, w_mix=w_out_b_b, mix_layer=j)
            xs = _mla_out(ol_s.astype(BF16).transpose(1, 0, 2), w_uv, w_out_b_b[j], g[1], xs, tm_s)
            mix_s = {}
        xp = _ffn(xp, g, w_ffn_in_b, w_ffn_out_b, l, tm_p, **mix_p)
        xs = _ffn(xs, g, w_ffn_in_b, w_ffn_out_b, l, tm_s, **mix_s)

    return (xp.reshape(batch, seq, d), xs.reshape(n_s, 1, d),
            jnp.stack(states_p), kv_p[0].reshape(batch, seq, KV_LORA), kv_p[1].reshape(batch, seq, ROPE_DIM),
            states_s, kv_s[0].reshape(n_s, 1, KV_LORA), kv_s[1].reshape(n_s, 1, ROPE_DIM))
```

```python
import functools

import jax
import jax.numpy as jnp
from jax import lax
from jax.experimental import pallas as pl
from jax.experimental.pallas import tpu as pltpu

F32 = jnp.float32
BF16 = jnp.bfloat16

D_MODEL = 1024
N_HEADS = 8
HEAD_DIM = 128
ROPE_DIM = 64
Q_LORA = 384
KV_LORA = 256
D_FF = 2816
N_HGRN = 2
EPS = 1e-6
LB_FLOOR = 1e-30
ROPE_THETA = 10000.0
ATTN_SCALE = (HEAD_DIM + ROPE_DIM) ** -0.5
ATTN_SCALE_LOG2 = ATTN_SCALE * 1.4426950408889634
MASK_VALUE = -1e30

HGRN_CHUNK = 256
FFN_COLS = 256
ATTN_TQ = 512
ATTN_TK = 512
ATTN_HEADS = 4
ATTN_SAMPLE_PAGES = 16
VMEM_LIMIT = 56 * 1024 * 1024
LANES = 128
SUBLANES = 8

_NT = (((1,), (1,)), ((), ()))
_TN = (((0,), (0,)), ((), ()))


def _params(*sem):
    return pltpu.CompilerParams(dimension_semantics=sem, vmem_limit_bytes=VMEM_LIMIT)


def _dot(a, b):
    return jnp.dot(a, b, preferred_element_type=F32)


def _dot_nt(a, b):
    return lax.dot_general(a, b, _NT, preferred_element_type=F32)


def _rms(xf, gain):
    ms = jnp.mean(xf * xf, axis=-1, keepdims=True)
    return xf * lax.rsqrt(ms + EPS) * gain


def _silu(x):
    return x * jax.nn.sigmoid(x)


def _const_spec(shape):
    n = len(shape)
    return pl.BlockSpec(shape, lambda *_: (0,) * n)


def _norm_mm_kernel(x_ref, g_ref, w_ref, o_ref, *, cols):
    h = _rms(x_ref[...], g_ref[...]).astype(BF16)
    for c in range(0, o_ref.shape[1], cols):
        o_ref[:, c:c + cols] = _dot(h, w_ref[:, c:c + cols])


def _layer_spec(w, layer):
    return pl.BlockSpec((None,) + w.shape[1:], lambda i: (layer, 0, 0))


def _norm_mm(x, gain, w, layer, tm):
    m, d = x.shape
    n = w.shape[2]
    return pl.pallas_call(
        functools.partial(_norm_mm_kernel, cols=1024),
        grid=(m // tm,),
        in_specs=[pl.BlockSpec((tm, d), lambda i: (i, 0)),
                  _const_spec((1, d)), _layer_spec(w, layer)],
        out_specs=pl.BlockSpec((tm, n), lambda i: (i, 0)),
        out_shape=jax.ShapeDtypeStruct((m, n), F32),
        compiler_params=_params("parallel"),
        name="norm_mm",
    )(x, gain.reshape(1, d), w)


def _ffn_kernel(*refs, with_mix):
    if with_mix:
        mix_ref, w_mix_ref, g_mix_ref = refs[:3]
        refs = refs[3:]
    x_ref, g_in_ref, g_out_ref, w_in_ref, w_out_ref, o_ref, a_scr = refs
    x = x_ref[...]
    if with_mix:
        x = x + _rms(_dot(mix_ref[...], w_mix_ref[...]), g_mix_ref[...])
    h = _rms(x, g_in_ref[...]).astype(BF16)
    for c in range(0, D_FF, FFN_COLS):
        g = _dot(h, w_in_ref[:, c:c + FFN_COLS])
        u = _dot(h, w_in_ref[:, D_FF + c:D_FF + c + FFN_COLS])
        a_scr[:, c:c + FFN_COLS] = (_silu(g) * u).astype(BF16)
    y = _dot(a_scr[...], w_out_ref[...])
    o_ref[...] = x + _rms(y, g_out_ref[...])


def _ffn(x, gains, w_in, w_out, layer, tm, mix=None, w_mix=None, mix_layer=None):
    m, d = x.shape
    with_mix = mix is not None
    in_specs = [pl.BlockSpec((tm, d), lambda i: (i, 0)),
                _const_spec((1, d)), _const_spec((1, d)),
                _layer_spec(w_in, layer), _layer_spec(w_out, layer)]
    args = [x, gains[2].reshape(1, d), gains[3].reshape(1, d), w_in, w_out]
    if with_mix:
        in_specs = [pl.BlockSpec((tm, mix.shape[1]), lambda i: (i, 0)),
                    _layer_spec(w_mix, mix_layer), _const_spec((1, d))] + in_specs
        args = [mix, w_mix, gains[1].reshape(1, d)] + args
    return pl.pallas_call(
        functools.partial(_ffn_kernel, with_mix=with_mix),
        grid=(m // tm,),
        in_specs=in_specs,
        out_specs=pl.BlockSpec((tm, d), lambda i: (i, 0)),
        out_shape=jax.ShapeDtypeStruct((m, d), F32),
        scratch_shapes=[pltpu.VMEM((tm, D_FF), BF16)],
        compiler_params=_params("parallel"),
        name="ffn",
    )(*args)


def _lower_bound(logits, layer):
    m = jnp.max(logits, axis=0, keepdims=True)
    e = jnp.exp(logits - m)
    p = e / jnp.sum(e, axis=0, keepdims=True)
    acc = p[0:1]
    for r in range(1, layer + 1):
        acc = acc + p[r:r + 1]
    return acc - p[0:1]


def _forget_gate(fz, lb):
    f = jnp.maximum(lb, LB_FLOOR) + (1.0 - lb) * jax.nn.sigmoid(fz)
    return jnp.minimum(f, 1.0)


def _mid_rows(b, m):
    c, d = b.shape
    if m >= SUBLANES:
        tiles = 2 * m // SUBLANES
        g4 = b.reshape(c // (2 * m), tiles, SUBLANES, d)
        src = g4[:, m // SUBLANES - 1, SUBLANES - 1:SUBLANES, :]
        tile = jnp.broadcast_to(src, (c // (2 * m), SUBLANES, d))
        return jnp.broadcast_to(tile[:, None], g4.shape).reshape(b.shape)
    g3 = b.reshape(c // SUBLANES, SUBLANES, d)
    if m == 4:
        return jnp.broadcast_to(g3[:, 3:4, :], g3.shape).reshape(b.shape)
    assert m == 2
    sub = lax.broadcasted_iota(jnp.int32, g3.shape, 1)
    out = jnp.where(sub < 4, jnp.broadcast_to(g3[:, 1:2, :], g3.shape),
                    jnp.broadcast_to(g3[:, 5:6, :], g3.shape))
    return out.reshape(b.shape)


def _hgrn_prompt_kernel(z_ref, lbl_ref, gn_ref, o_ref, st_ref, state_scr, *, layer):
    c = pl.program_id(1)
    chunk = z_ref.shape[0]

    @pl.when(c == 0)
    def _():
        state_scr[...] = jnp.zeros_like(state_scr)

    lb_all = _lower_bound(lbl_ref[...], layer)
    ti =lax.broadcasted_iota(jnp.int32, (chunk, chunk), 0)
    si = lax.broadcasted_iota(jnp.int32, (chunk, chunk), 1)
    level = jnp.where(ti > si, ti ^ si, 0)
    eye = ti == si
    tril = (ti >= si).astype(BF16)
    tril3 = jnp.concatenate([tril, tril, tril], axis=1)
    row = lax.broadcasted_iota(jnp.int32, (chunk, HEAD_DIM), 0)
    odd_row = (row & 1) != 0
    signs = []
    m = 1
    while m < chunk:
        signs.append((m, jnp.where((row & m) != 0, 1.0, -1.0)))
        m *= 2

    def intra_chunk(h):
        lo = h * HEAD_DIM
        hi = lo + HEAD_DIM
        qz = z_ref[:, lo:hi]
        fz = z_ref[:, D_MODEL + lo:D_MODEL + hi]
        f = _forget_gate(fz, lb_all[:, lo:hi])
        k = 1.0 - f
        q = _silu(qz) * (HEAD_DIM ** -0.5)
        g = jnp.log2(f)
        g_hi = g.astype(BF16)
        rest = g - g_hi.astype(F32)
        g_mid = rest.astype(BF16)
        g_lo = (rest - g_mid.astype(F32)).astype(BF16)
        b = _dot(tril3, jnp.concatenate([g_hi, g_mid, g_lo], axis=0))
        qb = q.astype(BF16)
        kb = k.astype(BF16)
        a = jnp.where(eye, _dot_nt(qb, kb), 0.0)
        for m, sign in signs:
            if m == 1:
                e = jnp.where(odd_row, f, 1.0).astype(BF16)
            else:
                e = jnp.exp2((b - _mid_rows(b, m)) * sign).astype(BF16)
            a = jnp.where(level >= m, _dot_nt(qb * e, kb * e), a)
        return q, k, b, a.astype(BF16)

    def read_out_and_update(h, q, k, b, a):
        lo = h * HEAD_DIM
        hi = lo + HEAD_DIM
        vb = z_ref[:, 2 * D_MODEL + lo:2 * D_MODEL + hi].astype(BF16)
        st = state_scr[h]
        o = _dot(a, vb) + _dot_nt((q * jnp.exp2(b)).astype(BF16), st.astype(BF16))
        b_last = b[chunk - 1:chunk, :]
        kd = (k * jnp.exp2(b_last - b)).astype(BF16)
        state_scr[h] = jnp.exp2(b_last) * st + lax.dot_general(vb, kd, _TN, preferred_element_type=F32)
        o = o * lax.rsqrt(jnp.mean(o * o, axis=-1, keepdims=True) + EPS)
        gz = z_ref[:, 3 * D_MODEL + lo:3 * D_MODEL + hi]
        o_ref[:, lo:hi] = (o * gn_ref[:, lo:hi] * _silu(gz)).astype(BF16)

    pending = intra_chunk(0)
    for h in range(N_HEADS):
        current = pending
        if h + 1 < N_HEADS:
            pending = intra_chunk(h + 1)
        read_out_and_update(h, *current)

    @pl.when(c == pl.num_programs(1) - 1)
    def _():
        for h in range(N_HEADS):
            st_ref[0, h] = state_scr[h].T


def _hgrn_prompt(z, lb_logits, g_norm, layer, batch, seq):
    n_chunks = seq // HGRN_CHUNK
    return pl.pallas_call(
        functools.partial(_hgrn_prompt_kernel, layer=layer),
        grid=(batch, n_chunks),
        in_specs=[pl.BlockSpec((HGRN_CHUNK, 4 * D_MODEL), lambda b, c: (b * n_chunks + c, 0)),
                  _const_spec(lb_logits.shape), _const_spec((1, D_MODEL))],
        out_specs=[pl.BlockSpec((HGRN_CHUNK, D_MODEL), lambda b, c: (b * n_chunks + c, 0)),
                   pl.BlockSpec((1, N_HEADS, HEAD_DIM, HEAD_DIM), lambda b, c: (b, 0, 0, 0))],
        out_shape=[jax.ShapeDtypeStruct((batch * seq, D_MODEL), BF16),
                   jax.ShapeDtypeStruct((batch, N_HEADS, HEAD_DIM, HEAD_DIM), F32)],
        scratch_shapes=[pltpu.VMEM((N_HEADS, HEAD_DIM, HEAD_DIM), F32)],
        compiler_params=_params("parallel", "arbitrary"),
        name="hgrn_prompt",
    )(z, lb_logits, g_norm.reshape(1, D_MODEL))


SAMPLE_GROUP = 4


def _hgrn_sample_kernel(*refs, layers, write_states):
    n = len(layers)
    z_refs = refs[:n]
    s_ref, lbl_ref, gn_ref, o_ref = refs[n:n + 4]
    so_ref = refs[n + 4] if write_states else None
    sb_scr, o_scr = refs[-2:]
    gn = gn_ref[...]
    for li, layer in enumerate(layers):
        z_ref = z_refs[li]
        lb = _lower_bound(lbl_ref[...], layer).reshape(N_HEADS, HEAD_DIM)
        z3 = z_ref[...].reshape(SAMPLE_GROUP * 4, N_HEADS, HEAD_DIM)
        f3 = _forget_gate(z3, lb[None])
        cols = f3.reshape(SAMPLE_GROUP * 4 * N_HEADS, HEAD_DIM).T
        mixes = li == n - 1
        for i in range(SAMPLE_GROUP):
            for h in range(N_HEADS):
                base = i * 4 * N_HEADS + N_HEADS + h
                f_col = cols[:, base:base + 1]
                v_row = z_ref[i, 2 * N_HEADS + h:2 * N_HEADS + h + 1, :]
                s_new = f_col * (s_ref[li, i, h] - v_row) + v_row
                if write_states:
                    so_ref[li, i, h] = s_new
                if mixes:
                    sb_scr[i * N_HEADS + h] = s_new.astype(BF16)
        if mixes:
            for i in range(SAMPLE_GROUP):
                q_rows = (_silu(z_ref[i, 0:N_HEADS, :]) * (HEAD_DIM ** -0.5)).astype(BF16)
                for h in range(N_HEADS):
                    r = i * N_HEADS + h
                    o_scr[r:r + 1, :] = _dot(q_rows, sb_scr[r])[h:h + 1, :]
            o = o_scr[...].reshape(SAMPLE_GROUP, N_HEADS, HEAD_DIM)
            o = o * lax.rsqrt(jnp.mean(o * o, axis=-1, keepdims=True) + EPS)
            o_ref[...] = o * gn[None] * _silu(z_ref[:, 3 * N_HEADS:4 * N_HEADS, :])


def _hgrn_sample(zs, state, lb_logits, g_norm, layer, write_states):
    n = zs[0].shape[0]
    layers = tuple(range(layer + 1)) if write_states else (layer,)
    assert len(zs) == len(layers)
    state_block = (len(layers), SAMPLE_GROUP, N_HEADS, HEAD_DIM, HEAD_DIM)
    first = layers[0]
    out_specs = [pl.BlockSpec((SAMPLE_GROUP, N_HEADS, HEAD_DIM), lambda i: (i, 0, 0))]
    out_shape = [jax.ShapeDtypeStruct((n, N_HEADS, HEAD_DIM), F32)]
    if write_states:
        out_specs.append(pl.BlockSpec(state_block, lambda i: (0, i, 0, 0, 0)))
        out_shape.append(jax.ShapeDtypeStruct((len(layers), n, N_HEADS, HEAD_DIM, HEAD_DIM), F32))
    outs = pl.pallas_call(
        functools.partial(_hgrn_sample_kernel, layers=layers, write_states=write_states),
        grid=(n // SAMPLE_GROUP,),
        in_specs=[pl.BlockSpec((SAMPLE_GROUP, 4 * N_HEADS, HEAD_DIM), lambda i: (i, 0, 0)) for _ in zs]
                 + [pl.BlockSpec(state_block, lambda i: (first, i, 0, 0, 0)),
                    _const_spec(lb_logits.shape), _const_spec((N_HEADS, HEAD_DIM))],
        out_specs=out_specs,
        out_shape=out_shape,
        scratch_shapes=[pltpu.VMEM((SAMPLE_GROUP * N_HEADS, HEAD_DIM, HEAD_DIM), BF16),
                        pltpu.VMEM((SAMPLE_GROUP * N_HEADS, HEAD_DIM), F32)],
        compiler_params=_params("parallel"),
        name="hgrn_sample",
    )(*[z.reshape(n, 4 * N_HEADS, HEAD_DIM) for z in zs], state, lb_logits, g_norm.reshape(N_HEADS, HEAD_DIM))
    o = outs[0].reshape(n, D_MODEL)
    return (o, outs[1]) if write_states else (o, None)


def _kv_kernel(x_ref, g_ref, wc_ref, wkr_ref, wkrs_ref, ga_ref, cs_ref, sn_ref, *rest, per_head):
    h = _rms(x_ref[...], g_ref[...]).astype(BF16)
    c = _rms(_dot(h, wc_ref[...]), ga_ref[...])
    kr = _dot(h, wkr_ref[...]) * cs_ref[...] + _dot(h, wkrs_ref[...]) * sn_ref[...]
    if not per_head:
        c_ref, kr_ref = rest
    else:
        wuk_ref, wuv_ref, c_ref, kr_ref, k_ref, v_ref = rest
        cb = c.astype(BF16)
        krb = kr.astype(BF16)
        k_nope = _dot(cb, wuk_ref[...]).astype(BF16)
        v = _dot(cb, wuv_ref[...]).astype(BF16)
        for hd in range(N_HEADS):
            k_ref[hd, :, 0:HEAD_DIM] = k_nope[:, hd * HEAD_DIM:(hd + 1) * HEAD_DIM]
            k_ref[hd, :, HEAD_DIM:HEAD_DIM + ROPE_DIM] = krb
            v_ref[hd] = v[:, hd * HEAD_DIM:(hd + 1) * HEAD_DIM]
    c_ref[...] = c
    kr_ref[...] = kr


def _kv_shared(x, gain, w_c, w_kr, w_kr_swapped, gain_a, cs, sn, tm, w_uk=None, w_uv=None):
    m, d = x.shape
    n_pos = cs.shape[0] // tm
    per_head = w_uk is not None
    in_specs = [pl.BlockSpec((tm, d), lambda i: (i, 0)),
                _const_spec((1, d)), _const_spec(w_c.shape), _const_spec(w_kr.shape),
                _const_spec(w_kr_swapped.shape), _const_spec((1, KV_LORA)),
                pl.BlockSpec((tm, ROPE_DIM), lambda i: (i % n_pos, 0)),
                pl.BlockSpec((tm, ROPE_DIM), lambda i: (i % n_pos, 0))]
    args = [x, gain.reshape(1, d), w_c, w_kr, w_kr_swapped, gain_a.reshape(1, KV_LORA), cs, sn]
    out_specs = [pl.BlockSpec((tm, KV_LORA), lambda i: (i, 0)),
                 pl.BlockSpec((tm, ROPE_DIM), lambda i: (i, 0))]
    out_shape = [jax.ShapeDtypeStruct((m, KV_LORA), F32), jax.ShapeDtypeStruct((m, ROPE_DIM), F32)]
    if per_head:
        in_specs += [_const_spec(w_uk.shape), _const_spec(w_uv.shape)]
        args += [w_uk, w_uv]
        out_specs += [pl.BlockSpec((N_HEADS, tm, HEAD_DIM + ROPE_DIM), lambda i: (0, i, 0)),
                      pl.BlockSpec((N_HEADS, tm, HEAD_DIM), lambda i: (0, i, 0))]
        out_shape += [jax.ShapeDtypeStruct((N_HEADS, m, HEAD_DIM + ROPE_DIM), BF16),
                      jax.ShapeDtypeStruct((N_HEADS, m, HEAD_DIM), BF16)]
    return pl.pallas_call(
        functools.partial(_kv_kernel, per_head=per_head),
        grid=(m // tm,),
        in_specs=in_specs,
        out_specs=out_specs,
        out_shape=out_shape,
        compiler_params=_params("parallel"),
        name="kv_shared",
    )(*args)


def _mla_q_kernel(x_ref, g_ref, wdq_ref, gq_ref, wn_ref, wr_ref, wrs_ref, cs_ref, sn_ref, *rest, absorbed):
    h = _rms(x_ref[...], g_ref[...]).astype(BF16)
    qa = _rms(_dot(h, wdq_ref[...]), gq_ref[...]).astype(BF16)
    q_nope = _dot(qa, wn_ref[...]).astype(BF16)
    reps = wr_ref.shape[1] // LANES
    cs = jnp.concatenate([cs_ref[...]] * reps, axis=1)
    sn = jnp.concatenate([sn_ref[...]] * reps, axis=1)
    q_rope = (_dot(qa, wr_ref[...]) * cs + _dot(qa, wrs_ref[...]) * sn).astype(BF16)
    for hd in range(N_HEADS):
        nope = q_nope[:, hd * HEAD_DIM:(hd + 1) * HEAD_DIM]
        rope = q_rope[:, hd * ROPE_DIM:(hd + 1) * ROPE_DIM]
        if absorbed:
            wukt_ref, ql_ref, qr_ref = rest
            ql_ref[hd] = _dot(nope, wukt_ref[hd]).astype(BF16)
            qr_ref[hd] = rope
        else:
            (q_ref,) = rest
            q_ref[hd, :, 0:HEAD_DIM] = nope
            q_ref[hd, :, HEAD_DIM:HEAD_DIM + ROPE_DIM] = rope


def _mla_q(x, gain, w_dq, gain_q, w_nope, w_rope, w_rope_swapped, cs, sn, tm, w_uk_t=None):
    m, d = x.shape
    n_pos = cs.shape[0] // tm
    absorbed = w_uk_t is not None
    in_specs = [pl.BlockSpec((tm, d), lambda i: (i, 0)),
                _const_spec((1, d)), _const_spec(w_dq.shape), _const_spec((1, Q_LORA)),
                _const_spec(w_nope.shape), _const_spec(w_rope.shape), _const_spec(w_rope_swapped.shape),
                pl.BlockSpec((tm, LANES), lambda i: (i % n_pos, 0)),
                pl.BlockSpec((tm, LANES), lambda i: (i % n_pos, 0))]
    args = [x, gain.reshape(1, d), w_dq, gain_q.reshape(1, Q_LORA), w_nope, w_rope, w_rope_swapped, cs, sn]
    if absorbed:
        in_specs.append(_const_spec(w_uk_t.shape))
        args.append(w_uk_t)
        widths = (KV_LORA, ROPE_DIM)
    else:
        widths = (HEAD_DIM + ROPE_DIM,)
    return pl.pallas_call(
        functools.partial(_mla_q_kernel, absorbed=absorbed),
        grid=(m // tm,),
        in_specs=in_specs,
        out_specs=[pl.BlockSpec((N_HEADS, tm, w), lambda i: (0, i, 0)) for w in widths],
        out_shape=[jax.ShapeDtypeStruct((N_HEADS, m, w), BF16) for w in widths],
        compiler_params=_params("parallel"),
        name="mla_q",
    )(*args)


def _attn_prompt_kernel(q_ref, k_ref, v_ref, o_ref, s_scr, m_scr, l_scr, acc_scr):
    i = pl.program_id(2)
    m_scr[...] = jnp.full_like(m_scr, MASK_VALUE)
    l_scr[...] = jnp.zeros_like(l_scr)
    acc_scr[...] = jnp.zeros_like(acc_scr)

    def key_start(j):
        return j * ATTN_TK if isinstance(j, int) else pl.multiple_of(j * ATTN_TK, ATTN_TK)

    def scores(j, slot):
        for hd in range(ATTN_HEADS):
            s_scr[slot, hd] = _dot_nt(q_ref[hd], k_ref[hd, pl.ds(key_start(j), ATTN_TK), :])

    def softmax_pv(j, slot, masked):
        start = key_start(j)
        for hd in range(ATTN_HEADS):
            s = s_scr[slot, hd]
            if masked:
                q_pos = i * ATTN_TQ + lax.broadcasted_iota(jnp.int32, s.shape, 0)
                k_pos = start + lax.broadcasted_iota(jnp.int32, s.shape, 1)
                s = jnp.where(k_pos <= q_pos, s, MASK_VALUE)
            tiles = [s[:, t:t + LANES] for t in range(0, ATTN_TK, LANES)]
            tile_max = functools.reduce(jnp.maximum, tiles)
            m_prev = m_scr[hd]
            m_new = jnp.maximum(m_prev, jnp.max(tile_max, axis=-1, keepdims=True))
            alpha = jnp.exp2((m_prev - m_new) * ATTN_SCALE_LOG2)
            ps = [jnp.exp2((t - m_new) * ATTN_SCALE_LOG2) for t in tiles]
            l_scr[hd] = alpha * l_scr[hd] + functools.reduce(jnp.add, ps)
            pv = _dot(jnp.concatenate(ps, axis=1).astype(BF16), v_ref[hd, pl.ds(start, ATTN_TK), :])
            acc_scr[hd] = alpha * acc_scr[hd] + pv
            m_scr[hd] = m_new

    def pair(jj, carry):
        j = 2 * jj
        scores(j + 1, 1)
        softmax_pv(j, 0, False)
        scores(j + 2, 0)
        softmax_pv(j + 1, 1, False)
        return carry

    scores(0, 0)
    lax.fori_loop(0, i // 2, pair, 0)

    @pl.when(i % 2 == 1)
    def _():
        scores(i, 1)
        softmax_pv(i - 1, 0, False)
        softmax_pv(i, 1, True)

    @pl.when(i % 2 == 0)
    def _():
        softmax_pv(i, 0, True)

    for hd in range(ATTN_HEADS):
        inv_l = 1.0 / jnp.sum(l_scr[hd], axis=-1, keepdims=True)
        o_ref[:, hd * HEAD_DIM:(hd + 1) * HEAD_DIM] = (acc_scr[hd] * inv_l).astype(BF16)


def _attn_prompt(q, k, v, batch, seq):
    assert ATTN_TQ == ATTN_TK
    nq = seq // ATTN_TQ
    width = q.shape[-1]
    return pl.pallas_call(
        _attn_prompt_kernel,
        grid=(batch, N_HEADS // ATTN_HEADS, nq),
        in_specs=[pl.BlockSpec((ATTN_HEADS, ATTN_TQ, width), lambda b, h, i: (h, b * nq + i, 0)),
                  pl.BlockSpec((ATTN_HEADS, seq, width), lambda b, h, i: (h, b, 0)),
                  pl.BlockSpec((ATTN_HEADS, seq, HEAD_DIM), lambda b, h, i: (h, b, 0))],
        out_specs=pl.BlockSpec((ATTN_TQ, ATTN_HEADS * HEAD_DIM), lambda b, h, i: (b * nq + i, h)),
        out_shape=jax.ShapeDtypeStruct((batch * seq, N_HEADS * HEAD_DIM), BF16),
        scratch_shapes=[pltpu.VMEM((2, ATTN_HEADS, ATTN_TQ, ATTN_TK), F32),
                        pltpu.VMEM((ATTN_HEADS, ATTN_TQ, LANES), F32),
                        pltpu.VMEM((ATTN_HEADS, ATTN_TQ, LANES), F32),
                        pltpu.VMEM((ATTN_HEADS, ATTN_TQ, HEAD_DIM), F32)],
        compiler_params=_params("parallel", "parallel", "arbitrary"),
        name="attn_prompt",
    )(q, k, v)


def _attn_sample_kernel(pt_ref, ql_ref, qr_ref, cn_ref, krn_ref, cache_c, cache_krt, o_ref,
                        c_buf, krt_buf, key_scr, s_scr, sem):
    b = pl.program_id(0)
    n_seq = pl.num_programs(0)
    n_pages, page = c_buf.shape[1], c_buf.shape[2]
    slot = b % 2

    def page_copies(seq, to_slot, p):
        pg = pt_ref[seq, p]
        return (pltpu.make_async_copy(cache_c.at[pg], c_buf.at[to_slot, p], sem.at[to_slot, 0]),
                pltpu.make_async_copy(cache_krt.at[pg], krt_buf.at[to_slot, p], sem.at[to_slot, 1]))

    def for_all_pages(seq, to_slot, action):
        def body(p, carry):
            for cp in page_copies(seq, to_slot, p):
                action(cp)
            return carry
        lax.fori_loop(0, n_pages, body, 0, unroll=8)

    @pl.when(b == 0)
    def _():
        for_all_pages(0, 0, lambda cp: cp.start())

    @pl.when(b + 1 < n_seq)
    def _():
        for_all_pages(b + 1, 1 - slot, lambda cp: cp.start())

    for_all_pages(b, slot, lambda cp: cp.wait())

    ql = ql_ref[0]
    qr = qr_ref[0]
    keys = ATTN_SAMPLE_PAGES * page
    for j in range(n_pages // ATTN_SAMPLE_PAGES):
        first = j * ATTN_SAMPLE_PAGES
        kc = c_buf[slot, first:first + ATTN_SAMPLE_PAGES].reshape(keys, KV_LORA).astype(BF16)
        key_scr[j * keys:(j + 1) * keys, :] = kc
        krt = jnp.concatenate([krt_buf[slot, first + p].astype(BF16) for p in range(ATTN_SAMPLE_PAGES)],
                              axis=1)
        s_scr[:, j * keys:(j + 1) * keys] = _dot_nt(ql, kc) + _dot(qr, krt)

    cn = cn_ref[0].astype(BF16).astype(F32)
    krn = krn_ref[0].astype(BF16).astype(F32)
    s_new = (jnp.sum(ql.astype(F32) * cn, axis=-1, keepdims=True)
             + jnp.sum(qr.astype(F32) * krn, axis=-1, keepdims=True))
    s = s_scr[...]
    m = jnp.maximum(s_new, jnp.max(s, axis=-1, keepdims=True))
    p = jnp.exp2((s - m) * ATTN_SCALE_LOG2)
    p_new = jnp.exp2((s_new - m) * ATTN_SCALE_LOG2)
    l = p_new + jnp.sum(p, axis=-1, keepdims=True)
    acc = p_new * cn + _dot(p.astype(BF16), key_scr[...])
    o_ref[0] = acc / l


def _attn_sample(ql, qr, c_new, kr_new, cache_c, cache_krt, page_table):
    n, n_pages = page_table.shape
    page = cache_c.shape[1]
    assert n_pages % ATTN_SAMPLE_PAGES == 0
    grid_spec = pltpu.PrefetchScalarGridSpec(
        num_scalar_prefetch=1,
        grid=(n,),
        in_specs=[pl.BlockSpec((1, N_HEADS, KV_LORA), lambda b, pt: (b, 0, 0)),
                  pl.BlockSpec((1, N_HEADS, ROPE_DIM), lambda b, pt: (b, 0, 0)),
                  pl.BlockSpec((1, 1, KV_LORA), lambda b, pt: (b, 0, 0)),
                  pl.BlockSpec((1, 1, ROPE_DIM), lambda b, pt: (b, 0, 0)),
                  pl.BlockSpec(memory_space=pl.ANY), pl.BlockSpec(memory_space=pl.ANY)],
        out_specs=pl.BlockSpec((1, N_HEADS, KV_LORA), lambda b, pt: (b, 0, 0)),
        scratch_shapes=[pltpu.VMEM((2, n_pages, page, KV_LORA), F32),
                        pltpu.VMEM((2, n_pages, ROPE_DIM, page), F32),
                        pltpu.VMEM((n_pages * page, KV_LORA), BF16),
                        pltpu.VMEM((N_HEADS, n_pages * page), F32),
                        pltpu.SemaphoreType.DMA((2, 2))],
    )
    return pl.pallas_call(
        _attn_sample_kernel,
        grid_spec=grid_spec,
        out_shape=jax.ShapeDtypeStruct((n, N_HEADS, KV_LORA), F32),
        compiler_params=_params("arbitrary"),
        name="attn_sample",
    )(page_table, ql, qr, c_new.reshape(n, 1, KV_LORA), kr_new.reshape(n, 1, ROPE_DIM), cache_c, cache_krt)


def _mla_out_kernel(ol_ref, wuv_ref, wout_ref, g_ref, x_ref, o_ref, o_scr):
    for hd in range(N_HEADS):
        o_scr[:, hd * HEAD_DIM:(hd + 1) * HEAD_DIM] = _dot(ol_ref[hd], wuv_ref[hd]).astype(BF16)
    y = _dot(o_scr[...], wout_ref[...])
    o_ref[...] = x_ref[...] + _rms(y, g_ref[...])


def _mla_out(o_lat, w_uv, w_out, gain, x, tm):
    m, d = x.shape
    return pl.pallas_call(
        _mla_out_kernel,
        grid=(m // tm,),
        in_specs=[pl.BlockSpec((N_HEADS, tm, KV_LORA), lambda i: (0, i, 0)),
                  _const_spec(w_uv.shape), _const_spec(w_out.shape), _const_spec((1, d)),
                  pl.BlockSpec((tm, d), lambda i: (i, 0))],
        out_specs=pl.BlockSpec((tm, d), lambda i: (i, 0)),
        out_shape=jax.ShapeDtypeStruct((m, d), F32),
        scratch_shapes=[pltpu.VMEM((tm, d), BF16)],
        compiler_params=_params("parallel"),
        name="mla_out",
    )(o_lat, w_uv, w_out, gain.reshape(1, d), x)


def _rope_tables(pos):
    half = ROPE_DIM // 2
    inv = ROPE_THETA ** (-jnp.arange(half, dtype=F32) / half)
    ang = pos.astype(F32)[:, None] * inv[None, :]
    cos, sin = jnp.cos(ang), jnp.sin(ang)
    return jnp.concatenate([cos, cos], axis=-1), jnp.concatenate([-sin, sin], axis=-1)


def _swap_halves(w):
    half = w.shape[-1] // 2
    return jnp.concatenate([w[..., half:], w[..., :half]], axis=-1)


def kernel(x_prompt, x_sample, state_hgrn, cache_kv_latent, cache_k_rope, page_table, norm_gains, w_ffn_in, w_ffn_out, w_in_a, lb_logits, g_norm_a, w_out_a, kv_norm, w_dkv, kv_a_norm, w_ukv, w_dq, q_a_norm, w_uq, w_out_b):
    batch, seq, d = x_prompt.shape
    n_s = x_sample.shape[0]
    depth = norm_gains.shape[0]
    n_hgrn = w_in_a.shape[0]
    past_len = page_table.shape[1] * cache_kv_latent.shape[1]
    tm_p, tm_s = 512, n_s

    w_ffn_in_b = w_ffn_in.astype(BF16)
    w_ffn_out_b = w_ffn_out.astype(BF16)
    w_in_a_b = w_in_a.astype(BF16)
    w_out_a_b = w_out_a.astype(BF16)
    w_out_b_b = w_out_b.astype(BF16)
    w_dq_b = w_dq.astype(BF16)
    w_c = w_dkv[:, :KV_LORA].astype(BF16)
    w_kr = w_dkv[:, KV_LORA:].astype(BF16)
    w_kr_sw = _swap_halves(w_kr)
    w_ukv_h = w_ukv.reshape(KV_LORA, N_HEADS, 2 * HEAD_DIM).astype(BF16)
    w_uk_t = w_ukv_h[:, :, :HEAD_DIM].transpose(1, 2, 0)
    w_uv = w_ukv_h[:, :, HEAD_DIM:].transpose(1, 0, 2)
    w_uk_all = w_ukv_h[:, :, :HEAD_DIM].reshape(KV_LORA, N_HEADS * HEAD_DIM)
    w_uv_all = w_ukv_h[:, :, HEAD_DIM:].reshape(KV_LORA, N_HEADS * HEAD_DIM)
    w_uq_h = w_uq.reshape(-1, Q_LORA, N_HEADS, HEAD_DIM + ROPE_DIM).astype(BF16)
    n_mla = w_uq_h.shape[0]
    w_q_nope = w_uq_h[..., :HEAD_DIM].reshape(n_mla, Q_LORA, N_HEADS * HEAD_DIM)
    w_q_rope = w_uq_h[..., HEAD_DIM:].reshape(n_mla, Q_LORA, N_HEADS * ROPE_DIM)
    w_q_rope_sw = _swap_halves(w_uq_h[..., HEAD_DIM:]).reshape(n_mla, Q_LORA, N_HEADS * ROPE_DIM)
    cache_krt = jnp.swapaxes(cache_k_rope, 1, 2)

    cs_p, sn_p = _rope_tables(jnp.arange(seq, dtype=jnp.int32))
    cs_s, sn_s = _rope_tables(jnp.full((n_s,), past_len, dtype=jnp.int32))
    cs_p8, sn_p8, cs_s8, sn_s8 = (jnp.tile(t, (1, LANES // ROPE_DIM)) for t in (cs_p, sn_p, cs_s, sn_s))

    xp = x_prompt.reshape(batch * seq, d)
    xs = x_sample.reshape(n_s, d)
    states_p, zs_all, states_s = [], [], None
    kv_p = kv_s = None
    for l in range(depth):
        g = norm_gains[l]
        if l < n_hgrn:
            zp = _norm_mm(xp, g[0], w_in_a_b, l, tm_p)
            zs = _norm_mm(xs, g[0], w_in_a_b, l, tm_s)
            op, st_p = _hgrn_prompt(zp, lb_logits, g_norm_a[l], l, batch, seq)
            zs_all.append(zs)
            last = l == n_hgrn - 1
            os_, states_s = _hgrn_sample(zs_all if last else [zs], state_hgrn, lb_logits, g_norm_a[l], l, last)
            states_p.append(st_p)
            mix_p = dict(mix=op, w_mix=w_out_a_b, mix_layer=l)
            mix_s = dict(mix=os_.astype(BF16), w_mix=w_out_a_b, mix_layer=l)
        else:
            j = l - n_hgrn
            if kv_p is None:
                kv_p = _kv_shared(xp, kv_norm, w_c, w_kr, w_kr_sw, kv_a_norm, cs_p, sn_p, tm_p,
                                  w_uk=w_uk_all, w_uv=w_uv_all)
                kv_s = _kv_shared(xs, kv_norm, w_c, w_kr, w_kr_sw, kv_a_norm, cs_s, sn_s, tm_s)
                k_p, v_p = kv_p[2], kv_p[3]
            q_args = (g[0], w_dq_b[j], q_a_norm[j], w_q_nope[j], w_q_rope[j], w_q_rope_sw[j])
            (q_p,) = _mla_q(xp, *q_args, cs_p8, sn_p8, tm_p)
            ql_s, qr_s = _mla_q(xs, *q_args, cs_s8, sn_s8, tm_s, w_uk_t=w_uk_t)
            o_p = _attn_prompt(q_p, k_p, v_p, batch, seq)
            ol_s = _attn_sample(ql_s.transpose(1, 0, 2), qr_s.transpose(1, 0, 2), kv_s[0], kv_s[1],
                                cache_kv_latent, cache_krt, page_table)
            mix_p = dict(mix=o_p, w_mix=w_out_b_b, mix_layer=j)
            xs = _mla_out(ol_s.astype(BF16).transpose(1, 0, 2), w_uv, w_out_b_b[j], g[1], xs, tm_s)
            mix_s = {}
        xp = _ffn(xp, g, w_ffn_in_b, w_ffn_out_b, l, tm_p, **mix_p)
        xs = _ffn(xs, g, w_ffn_in_b, w_ffn_out_b, l, tm_s, **mix_s)

    return (xp.reshape(batch, seq, d), xs.reshape(n_s, 1, d),
            jnp.stack(states_p), kv_p[0].reshape(batch, seq, KV_LORA), kv_p[1].reshape(batch, seq, ROPE_DIM),
            states_s, kv_s[0].reshape(n_s, 1, KV_LORA), kv_s[1].reshape(n_s, 1, ROPE_DIM))
```
